```python
import jax
import jax.numpy as jnp
from jax import lax
import numpy as np

D_MODEL = 2048
BATCH = 1
SEQ = 8192
DEPTH = 1
DEC_BATCH = 8
DEC_SEQ = 16
PAST_LEN = 2048

CHUNK = 64
DN_QK_HEADS = 16
DN_V_HEADS = 32
DN_HEAD_DK = 128
DN_HEAD_DV = 128
DN_QK_DIM = DN_QK_HEADS * DN_HEAD_DK
DN_V_DIM = DN_V_HEADS * DN_HEAD_DV
DN_QKV_DIM = 2 * DN_QK_DIM + DN_V_DIM
CONV_W = 4
LRU_W = D_MODEL
LRU_HEADS = 16
LRU_BW = LRU_W // LRU_HEADS
LRU_C = 8.0
N_EXPERTS = 32
TOP_K = 4
D_FF = D_MODEL
SWIGLU_LIMIT = 7.0
SWIGLU_ALPHA = 1.702
MOE_BLOCK = 128
DEEPNORM_ALPHA = (2.0 * DEPTH) ** 0.25
DEEPNORM_BETA = (8.0 * DEPTH) ** -0.25
LN_EPS = 1e-5
NORM_EPS = 1e-6
OFF_Z = DN_QKV_DIM
OFF_B = OFF_Z + DN_V_DIM
OFF_A = OFF_B + DN_V_HEADS
OFF_LX = OFF_A + DN_V_HEADS
OFF_LY = OFF_LX + LRU_W
OFF_GA = OFF_LY + LRU_W
OFF_GB = OFF_GA + D_MODEL
IN_DIM = OFF_GB + D_MODEL

kernel_name = 'hybrid_gdn_rglru_moe_stream_step'

F32 = jnp.float32


def layer_norm(x, g, b):
    xf = x.astype(F32)
    mu = jnp.mean(xf, axis=-1, keepdims=True)
    var = jnp.mean(jnp.square(xf - mu), axis=-1, keepdims=True)
    return ((xf - mu) * lax.rsqrt(var + LN_EPS) * g + b).astype(x.dtype)


def causal_dwconv(x, prev, w, b):
    t = x.shape[1]
    xp = jnp.concatenate([prev.astype(x.dtype), x], axis=1)
    y = sum(xp[:, j:j + t] * w[j] for j in range(CONV_W))
    if b is not None:
        y = y + b
    return y, xp[:, t:]


def l2norm(u):
    return u * lax.rsqrt(jnp.sum(u * u, axis=-1, keepdims=True) + NORM_EPS)


def chunk_gated_delta(q, k, v, g, beta, s0):
    bsz, t, h, _ = q.shape
    dv = v.shape[-1]
    n = -(-t // CHUNK)
    pad = n * CHUNK - t

    def to_chunks(u):
        u = jnp.pad(u, [(0, 0), (0, pad)] + [(0, 0)] * (u.ndim - 2))
        u = u.reshape((bsz, n, CHUNK) + u.shape[2:])
        return jnp.moveaxis(u, 3, 1)

    q, k, v, g, beta = (to_chunks(u) for u in (q, k, v, g, beta))
    gc = jnp.cumsum(g, axis=-1)
    idx = jnp.arange(CHUNK)
    causal = idx[:, None] >= idx[None, :]
    strict = idx[:, None] > idx[None, :]
    decay = jnp.exp(jnp.where(causal, gc[..., :, None] - gc[..., None, :], -jnp.inf))
    kb = k * beta[..., None]
    lmat = jnp.where(strict, jnp.einsum('bhncd,bhnsd->bhncs', kb, k) * decay, 0.0)
    rhs = jnp.concatenate([v * beta[..., None], kb * jnp.exp(gc)[..., None]], axis=-1)
    sol = lax.linalg.triangular_solve(lmat, rhs, left_side=True, lower=True, unit_diagonal=True)
    u_base, k_cum = sol[..., :dv], sol[..., dv:]
    attn = jnp.einsum('bhncd,bhnsd->bhncs', q, k) * decay
    q_dec = q * jnp.exp(gc)[..., None]
    g_last = gc[..., -1]
    k_dec = k * jnp.exp(g_last[..., None] - gc)[..., None]

    def step(s, inp):
        attn_i, qd_i, ub_i, kc_i, kd_i, gl_i = inp
        u = ub_i - jnp.einsum('bhcd,bhde->bhce', kc_i, s)
        o = jnp.einsum('bhcd,bhde->bhce', qd_i, s) + jnp.einsum('bhcs,bhse->bhce', attn_i, u)
        s = s * jnp.exp(gl_i)[..., None, None] + jnp.einsum('bhcd,bhce->bhde', kd_i, u)
        return s, o

    xs = tuple(jnp.moveaxis(u, 2, 0) for u in (attn, q_dec, u_base, k_cum, k_dec, g_last))
    s_fin, o = lax.scan(step, s0.astype(F32), xs)
    o = jnp.transpose(o, (1, 0, 3, 2, 4)).reshape(bsz, n * CHUNK, h, dv)[:, :t]
    return o, s_fin


def gated_deltanet(qkv_raw, z, b, a, conv_prev, s0, w_conv, a_log, dt_bias, norm_w):
    bsz, t, _ = qkv_raw.shape
    qkv, conv_new = causal_dwconv(qkv_raw.astype(F32), conv_prev, w_conv, None)
    qkv = jax.nn.silu(qkv)
    q = qkv[..., :DN_QK_DIM].reshape(bsz, t, DN_QK_HEADS, DN_HEAD_DK)
    k = qkv[..., DN_QK_DIM:2 * DN_QK_DIM].reshape(bsz, t, DN_QK_HEADS, DN_HEAD_DK)
    v = qkv[..., 2 * DN_QK_DIM:].reshape(bsz, t, DN_V_HEADS, DN_HEAD_DV)
    rep = DN_V_HEADS // DN_QK_HEADS
    q = jnp.repeat(l2norm(q) * (DN_HEAD_DK ** -0.5), rep, axis=2)
    k = jnp.repeat(l2norm(k), rep, axis=2)
    beta = jax.nn.sigmoid(b.astype(F32))
    g = -jnp.exp(a_log.astype(F32)) * jax.nn.softplus(a.astype(F32) + dt_bias)
    o, s_new = chunk_gated_delta(q, k, v, g, beta, s0)
    zg = jax.nn.silu(z.astype(F32).reshape(bsz, t, DN_V_HEADS, DN_HEAD_DV))
    o = o * lax.rsqrt(jnp.mean(o * o, axis=-1, keepdims=True) + NORM_EPS) * norm_w * zg
    return o.reshape(bsz, t, DN_V_DIM), conv_new, s_new


def rglru_branch(x_in, y_in, conv_prev, h0, w_conv, b_conv, w_ga, b_ga, w_gx, b_gx, lam):
    xc, conv_new = causal_dwconv(x_in.astype(F32), conv_prev, w_conv, b_conv)
    bsz, t, _ = xc.shape
    xh = xc.reshape(bsz, t, LRU_HEADS, LRU_BW)
    r = jax.nn.sigmoid(jnp.einsum('btnc,ncd->btnd', xh, w_ga) + b_ga).reshape(bsz, t, LRU_W)
    i = jax.nn.sigmoid(jnp.einsum('btnc,ncd->btnd', xh, w_gx) + b_gx).reshape(bsz, t, LRU_W)
    log_a = -LRU_C * r * jax.nn.softplus(-lam.astype(F32))
    a = jnp.exp(log_a)
    bx = jnp.sqrt(-jnp.expm1(2.0 * log_a)) * (i * xc)
    bx = bx.at[:, 0].add(a[:, 0] * h0.astype(F32))
    _, h = lax.associative_scan(lambda e1, e2: (e1[0] * e2[0], e2[0] * e1[1] + e2[1]), (a, bx), axis=1)
    out = h * jax.nn.gelu(y_in.astype(F32))
    return out, conv_new, h[:, -1]


def moe_ffn(x, w_router, b_router, w_gu, b_gu, w_dn, b_dn):
    bsz, t, d = x.shape
    xf = x.reshape(-1, d)
    n = xf.shape[0]
    nk = n * TOP_K
    logits = xf.astype(F32) @ w_router.astype(F32) + b_router.astype(F32)
    top_logit, top_e = lax.top_k(logits, TOP_K)
    gate = jax.nn.softmax(top_logit, axis=-1)
    flat_e = top_e.reshape(nk)
    order = jnp.argsort(flat_e)
    sorted_e = flat_e[order]
    sorted_tok = order // TOP_K
    sorted_gate = gate.reshape(nk)[order]
    counts = jnp.bincount(flat_e, length=N_EXPERTS)
    padded = (counts + MOE_BLOCK - 1) // MOE_BLOCK * MOE_BLOCK
    start = jnp.cumsum(counts) - counts
    pend = jnp.cumsum(padded)
    pstart = pend - padded
    dest = pstart[sorted_e] + jnp.arange(nk) - start[sorted_e]
    n_blocks = (nk + N_EXPERTS * (MOE_BLOCK - 1) + MOE_BLOCK - 1) // MOE_BLOCK
    rows = jnp.zeros((n_blocks * MOE_BLOCK, d), xf.dtype).at[dest].set(xf[sorted_tok])
    block_e = jnp.minimum(jnp.searchsorted(pend, jnp.arange(n_blocks) * MOE_BLOCK, side='right'), N_EXPERTS - 1)

    def expert_block(args):
        xb, e = args
        hgu = xb @ w_gu[e] + b_gu[e]
        hg = jnp.minimum(hgu[:, :D_FF], SWIGLU_LIMIT)
        hu = jnp.clip(hgu[:, D_FF:], -SWIGLU_LIMIT, SWIGLU_LIMIT)
        act = (hu + 1.0) * hg * jax.nn.sigmoid(SWIGLU_ALPHA * hg)
        return act @ w_dn[e] + b_dn[e]

    out = lax.map(expert_block, (rows.reshape(n_blocks, MOE_BLOCK, d), block_e)).reshape(-1, d)
    y = jax.ops.segment_sum(out[dest].astype(F32) * sorted_gate[:, None], sorted_tok, num_segments=n)
    return y.reshape(bsz, t, d).astype(x.dtype)


def hybrid_layer(x, dn_conv, dn_s, lru_conv, lru_h,
                 w_in, w_dn_conv, dn_a_log, dn_dt_bias, dn_norm_w,
                 w_lru_conv, b_lru_conv, w_lru_ga, b_lru_ga, w_lru_gx, b_lru_gx, lru_lambda,
                 w_proj_a, w_proj_b, w_out, ln1_g, ln1_b,
                 w_router, b_router, w_gate_up, b_gate_up, w_down, b_down, ln2_g, ln2_b):
    p = x @ w_in
    o_a, dn_conv_new, dn_s_new = gated_deltanet(
        p[..., :OFF_Z], p[..., OFF_Z:OFF_B], p[..., OFF_B:OFF_A], p[..., OFF_A:OFF_LX],
        dn_conv, dn_s, w_dn_conv, dn_a_log, dn_dt_bias, dn_norm_w)
    o_b, lru_conv_new, lru_h_new = rglru_branch(
        p[..., OFF_LX:OFF_LY], p[..., OFF_LY:OFF_GA], lru_conv, lru_h,
        w_lru_conv, b_lru_conv, w_lru_ga, b_lru_ga, w_lru_gx, b_lru_gx, lru_lambda)
    merged = (jax.nn.sigmoid(p[..., OFF_GA:OFF_GB]) * (o_a.astype(x.dtype) @ w_proj_a)
              + jax.nn.sigmoid(p[..., OFF_GB:]) * (o_b.astype(x.dtype) @ w_proj_b))
    h = layer_norm(DEEPNORM_ALPHA * x + merged.astype(x.dtype) @ w_out, ln1_g, ln1_b)
    h = layer_norm(DEEPNORM_ALPHA * h + moe_ffn(h, w_router, b_router, w_gate_up, b_gate_up, w_down, b_down),
                   ln2_g, ln2_b)
    return h, dn_conv_new, dn_s_new, lru_conv_new, lru_h_new


def setup_inputs(seed: int = 0) -> dict:
    key = jax.random.key(seed)
    ks = jax.random.split(key, 40)

    def nrm(i, shape, s):
        return jax.random.normal(ks[i], shape, F32) * s

    L = DEPTH
    u_dt = jax.random.uniform(ks[10], (L, DN_V_HEADS), F32)
    dt = jnp.exp(u_dt * (np.log(0.1) - np.log(0.001)) + np.log(0.001))
    u_a = jax.random.uniform(ks[17], (L, LRU_W), F32, minval=0.9, maxval=0.999)
    a_base = u_a ** (1.0 / LRU_C)
    return {
        'x_prompt': nrm(0, (BATCH, SEQ, D_MODEL), 1.0),
        'x_sample': nrm(1, (DEC_BATCH, DEC_SEQ, D_MODEL), 1.0),
        'state_dn_conv': nrm(2, (L, DEC_BATCH, CONV_W - 1, DN_QKV_DIM), 1.0),
        'state_dn_ssm': nrm(3, (L, DEC_BATCH, DN_V_HEADS, DN_HEAD_DK, DN_HEAD_DV), 0.1),
        'state_lru_conv': nrm(4, (L, DEC_BATCH, CONV_W - 1, LRU_W), 1.0),
        'state_lru_h': nrm(5, (L, DEC_BATCH, LRU_W), 0.5),
        'w_in': nrm(6, (L, D_MODEL, IN_DIM), D_MODEL ** -0.5),
        'w_dn_conv': nrm(7, (L, CONV_W, DN_QKV_DIM), CONV_W ** -0.5),
        'dn_a_log': jnp.log(jax.random.uniform(ks[8], (L, DN_V_HEADS), F32, minval=1.0, maxval=16.0)),
        'dn_dt_bias': dt + jnp.log(-jnp.expm1(-dt)),
        'dn_norm_w': 1.0 + nrm(11, (L, DN_HEAD_DV), 0.02),
        'w_lru_conv': nrm(12, (L, CONV_W, LRU_W), CONV_W ** -0.5),
        'b_lru_conv': nrm(13, (L, LRU_W), 0.01),
        'w_lru_ga': nrm(14, (L, LRU_HEADS, LRU_BW, LRU_BW), LRU_BW ** -0.5),
        'b_lru_ga': nrm(15, (L, LRU_HEADS, LRU_BW), 0.01),
        'w_lru_gx': nrm(16, (L, LRU_HEADS, LRU_BW, LRU_BW), LRU_BW ** -0.5),
        'b_lru_gx': nrm(18, (L, LRU_HEADS, LRU_BW), 0.01),
        'lru_lambda': jnp.log(a_base) - jnp.log1p(-a_base),
        'w_proj_a': nrm(19, (L, DN_V_DIM, D_MODEL), DN_V_DIM ** -0.5),
        'w_proj_b': nrm(20, (L, LRU_W, D_MODEL), LRU_W ** -0.5),
        'w_out': nrm(21, (L, D_MODEL, D_MODEL), DEEPNORM_BETA * D_MODEL ** -0.5),
        'ln1_g': 1.0 + nrm(22, (L, D_MODEL), 0.02),
        'ln1_b': nrm(23, (L, D_MODEL), 0.01),
        'w_router': nrm(24, (L, D_MODEL, N_EXPERTS), D_MODEL ** -0.5),
        'b_router': nrm(25, (L, N_EXPERTS), 0.01),
        'w_gate_up': nrm(26, (L, N_EXPERTS, D_MODEL, 2 * D_FF), D_MODEL ** -0.5),
        'b_gate_up': nrm(27, (L, N_EXPERTS, 2 * D_FF), 0.01),
        'w_down': nrm(28, (L, N_EXPERTS, D_FF, D_MODEL), DEEPNORM_BETA * D_FF ** -0.5),
        'b_down': nrm(29, (L, N_EXPERTS, D_MODEL), 0.01),
        'ln2_g': 1.0 + nrm(30, (L, D_MODEL), 0.02),
        'ln2_b': nrm(31, (L, D_MODEL), 0.01),
    }


def reference(x_prompt, x_sample, state_dn_conv, state_dn_ssm, state_lru_conv, state_lru_h,
              w_in, w_dn_conv, dn_a_log, dn_dt_bias, dn_norm_w,
              w_lru_conv, b_lru_conv, w_lru_ga, b_lru_ga, w_lru_gx, b_lru_gx, lru_lambda,
              w_proj_a, w_proj_b, w_out, ln1_g, ln1_b,
              w_router, b_router, w_gate_up, b_gate_up, w_down, b_down, ln2_g, ln2_b):
    bp = x_prompt.shape[0]
    xp, xs = x_prompt, x_sample
    sp = ([], [], [], [])
    ss = ([], [], [], [])
    for l in range(DEPTH):
        lw = (w_in[l], w_dn_conv[l], dn_a_log[l], dn_dt_bias[l], dn_norm_w[l],
              w_lru_conv[l], b_lru_conv[l], w_lru_ga[l], b_lru_ga[l], w_lru_gx[l], b_lru_gx[l], lru_lambda[l],
              w_proj_a[l], w_proj_b[l], w_out[l], ln1_g[l], ln1_b[l],
              w_router[l], b_router[l], w_gate_up[l], b_gate_up[l], w_down[l], b_down[l], ln2_g[l], ln2_b[l])
        xp, *st_p = hybrid_layer(
            xp,
            jnp.zeros((bp, CONV_W - 1, DN_QKV_DIM), F32),
            jnp.zeros((bp, DN_V_HEADS, DN_HEAD_DK, DN_HEAD_DV), F32),
            jnp.zeros((bp, CONV_W - 1, LRU_W), F32),
            jnp.zeros((bp, LRU_W), F32),
            *lw)
        xs, *st_s = hybrid_layer(xs, state_dn_conv[l], state_dn_ssm[l], state_lru_conv[l], state_lru_h[l], *lw)
        for j in range(4):
            sp[j].append(st_p[j])
            ss[j].append(st_s[j])
    return (xp, xs,
            jnp.stack(sp[0]), jnp.stack(sp[1]), jnp.stack(sp[2]), jnp.stack(sp[3]),
            jnp.stack(ss[0]), jnp.stack(ss[1]), jnp.stack(ss[2]), jnp.stack(ss[3]))
```

```python
import functools

import jax
import jax.numpy as jnp
from jax import lax
from jax.experimental import pallas as pl
from jax.experimental.pallas import tpu as pltpu

F32 = jnp.float32
BF16 = jnp.bfloat16

D_MODEL = 2048
DN_QK_HEADS = 16
DN_V_HEADS = 32
DN_HEAD = 128
DN_QK_DIM = DN_QK_HEADS * DN_HEAD
DN_V_DIM = DN_V_HEADS * DN_HEAD
DN_QKV_DIM = 2 * DN_QK_DIM + DN_V_DIM
CONV_W = 4
LRU_W = D_MODEL
LRU_HEADS = 16
LRU_BW = LRU_W // LRU_HEADS
LRU_C = 8.0
N_EXPERTS = 32
TOP_K = 4
D_FF = D_MODEL
SWIGLU_LIMIT = 7.0
SWIGLU_ALPHA = 1.702
DEEPNORM_ALPHA = 2.0 ** 0.25
LN_EPS = 1e-5
NORM_EPS = 1e-6
OFF_Z = DN_QKV_DIM
OFF_B = OFF_Z + DN_V_DIM
OFF_A = OFF_B + DN_V_HEADS
OFF_LX = OFF_A + DN_V_HEADS
OFF_LY = OFF_LX + LRU_W
OFF_GA = OFF_LY + LRU_W
OFF_GB = OFF_GA + D_MODEL

V7X_VMEM_LIMIT_BYTES = 56 * 1024 * 1024
SUBLANES = 8

GDN_CHUNK = 64
GDN_CHUNKS_PER_STEP = 8
MOE_GROUP_ROWS = 1280
MOE_SUB_ROWS = 256
MOE_FF_TILE = 256


def _params(n_axes, vmem_bytes):
    limit = min(V7X_VMEM_LIMIT_BYTES, int(vmem_bytes * 1.2) + (4 << 20))
    return pltpu.CompilerParams(dimension_semantics=("arbitrary",) * n_axes, vmem_limit_bytes=limit)


def _mm_kernel(x_ref, w_ref, o_ref):
    o_ref[...] = jnp.dot(x_ref[...], w_ref[...].astype(BF16),
                         preferred_element_type=F32).astype(o_ref.dtype)


def _matmul(x_bf, w, *, tm, tn, n_cols=None):
    m, k = x_bf.shape
    n = w.shape[1] if n_cols is None else n_cols
    assert m % tm == 0 and n % tn == 0
    vmem = 2 * (tm * k * 2 + k * tn * 4 + tm * tn * 4) + k * tn * 2
    return pl.pallas_call(
        _mm_kernel,
        grid=(m // tm, n // tn),
        in_specs=[pl.BlockSpec((tm, k), lambda i, j: (i, 0)),
                  pl.BlockSpec((k, tn), lambda i, j: (0, j))],
        out_specs=pl.BlockSpec((tm, tn), lambda i, j: (i, j)),
        out_shape=jax.ShapeDtypeStruct((m, n), F32),
        compiler_params=_params(2, vmem),
        name="dense_matmul",
    )(x_bf, w)


def _dot_nt(a, b):
    return lax.dot_general(a, b, (((1,), (1,)), ((), ())), preferred_element_type=F32)


def _dot_tn(a, b):
    return lax.dot_general(a, b, (((0,), (0,)), ((), ())), preferred_element_type=F32)


def _gdn_kernel(q_ref, k_ref, v_ref, zg_ref, g_ref, beta_ref, s0_ref, nw_ref, o_ref, s_ref, *, chunk, n_chunks):
    c = chunk

    @pl.when(pl.program_id(2) == 0)
    def _():
        s_ref[...] = s0_ref[...]

    ri = lax.broadcasted_iota(jnp.int32, (c, c), 0)
    ci = lax.broadcasted_iota(jnp.int32, (c, c), 1)
    causal = ri >= ci
    strict = ri > ci
    eye = ri == ci
    eye_f = jnp.where(eye, 1.0, 0.0).astype(F32)
    n_double = c.bit_length() - 2
    nw = nw_ref[...]

    def col_of(row):
        return jnp.sum(jnp.where(eye, jnp.broadcast_to(row, (c, c)), 0.0), axis=1, keepdims=True)

    def one_chunk(ic, carry):
        r0 = pl.multiple_of(ic * c, c)
        q = q_ref[0, pl.ds(r0, c), :]
        k = k_ref[0, pl.ds(r0, c), :]
        k_bf = k.astype(BF16)
        kq = _dot_nt(jnp.concatenate([k_bf, q.astype(BF16)], axis=0), k_bf)
        kk = kq[:c]
        qk = kq[c:]
        for hh in range(2):
            lanes = slice(hh * DN_HEAD, (hh + 1) * DN_HEAD)
            g_row = g_ref[0, 0, hh, pl.ds(ic, 1), :]
            beta_col = col_of(beta_ref[0, 0, hh, pl.ds(ic, 1), :])
            g_col = col_of(g_row)
            gc_col = jnp.sum(jnp.where(causal, jnp.broadcast_to(g_row, (c, c)), 0.0), axis=1, keepdims=True)
            gc_row = jnp.sum(jnp.where(ri <= ci, jnp.broadcast_to(g_col, (c, c)), 0.0), axis=0, keepdims=True)
            decay = jnp.where(causal, jnp.exp(jnp.where(causal, gc_col - gc_row, 0.0)), 0.0)
            x = jnp.where(strict, -(beta_col * kk * decay), 0.0)
            attn = qk * decay
            t_inv = eye_f + x
            x_bf = x.astype(BF16)
            p = jnp.dot(x_bf, x_bf, preferred_element_type=F32)
            for it in range(n_double):
                p_bf = p.astype(BF16)
                if it + 1 < n_double:
                    both = jnp.dot(p_bf, jnp.concatenate([p_bf, t_inv.astype(BF16)], axis=1),
                                   preferred_element_type=F32)
                    p = both[:, :c]
                    t_inv = t_inv + both[:, c:]
                else:
                    t_inv = t_inv + jnp.dot(p_bf, t_inv.astype(BF16), preferred_element_type=F32)
            v = v_ref[0, pl.ds(r0, c), lanes]
            egc = jnp.exp(gc_col)
            g_last = gc_col[c - 1:c, :]
            kb = k * beta_col
            rhs = jnp.concatenate([v * beta_col, kb * egc], axis=1).astype(BF16)
            sol = jnp.dot(t_inv.astype(BF16), rhs, preferred_element_type=F32)
            u_base = sol[:, :DN_HEAD]
            k_cum = sol[:, DN_HEAD:]
            s = s_ref[0, hh]
            s_bf = s.astype(BF16)
            lhs = jnp.concatenate([k_cum, q * egc], axis=0).astype(BF16)
            ks_qs = jnp.dot(lhs, s_bf, preferred_element_type=F32)
            u = u_base - ks_qs[:c]
            u_bf = u.astype(BF16)
            o = ks_qs[c:] + jnp.dot(attn.astype(BF16), u_bf, preferred_element_type=F32)
            k_dec = (k * jnp.exp(g_last - gc_col)).astype(BF16)
            s_ref[0, hh] = s * jnp.exp(g_last) + _dot_tn(k_dec, u_bf)
            o = o * lax.rsqrt(jnp.mean(o * o, axis=-1, keepdims=True) + NORM_EPS) * nw
            o_ref[0, pl.ds(r0, c), lanes] = (o * zg_ref[0, pl.ds(r0, c), lanes]).astype(o_ref.dtype)
        return carry

    lax.fori_loop(0, n_chunks, one_chunk, 0)


def _gated_delta(q, k, v, zg, g5, beta5, s0, norm_w, *, chunk, n_chunks):
    b, t, _ = q.shape
    rows = chunk * n_chunks
    assert t % rows == 0
    qk_spec = pl.BlockSpec((1, rows, DN_HEAD), lambda ib, ih, it: (ib, it, ih))
    v_spec = pl.BlockSpec((1, rows, 2 * DN_HEAD), lambda ib, ih, it: (ib, it, ih))
    g_spec = pl.BlockSpec((1, 1, 2, n_chunks, chunk), lambda ib, ih, it: (ib, ih, 0, it, 0))
    s_spec = pl.BlockSpec((1, 2, DN_HEAD, DN_HEAD), lambda ib, ih, it: (ib, ih, 0, 0))
    vmem = 2 * (2 * rows * DN_HEAD * 4 + 2 * rows * 2 * DN_HEAD * 4 + rows * 2 * DN_HEAD * 2
                + 4 * DN_HEAD * DN_HEAD * 4) + (8 << 20)
    return pl.pallas_call(
        functools.partial(_gdn_kernel, chunk=chunk, n_chunks=n_chunks),
        grid=(b, DN_QK_HEADS, t // rows),
        in_specs=[qk_spec, qk_spec, v_spec, v_spec, g_spec, g_spec, s_spec,
                  pl.BlockSpec((1, DN_HEAD), lambda ib, ih, it: (0, 0))],
        out_specs=[v_spec, s_spec],
        out_shape=[jax.ShapeDtypeStruct((b, t, DN_V_DIM), BF16),
                   jax.ShapeDtypeStruct((b, DN_V_HEADS, DN_HEAD, DN_HEAD), F32)],
        compiler_params=_params(3, vmem),
        name="gated_delta_rule",
    )(q, k, v, zg, g5, beta5, s0, norm_w)


def _lru_kernel(xc_ref, gy_ref, wga_ref, bga_ref, wgx_ref, bgx_ref, sp_ref, h0_ref, o_ref, hl_ref,
                a_s, b_s, h_s, *, rows):
    @pl.when(pl.program_id(1) == 0)
    def _():
        h_s[...] = h0_ref[0]

    for hd in range(LRU_HEADS):
        lanes = slice(hd * LRU_BW, (hd + 1) * LRU_BW)
        xh = xc_ref[0, :, lanes]
        xh_bf = xh.astype(BF16)
        r = jax.nn.sigmoid(jnp.dot(xh_bf, wga_ref[hd].astype(BF16), preferred_element_type=F32) + bga_ref[hd])
        i = jax.nn.sigmoid(jnp.dot(xh_bf, wgx_ref[hd].astype(BF16), preferred_element_type=F32) + bgx_ref[hd])
        log_a = (-LRU_C) * r * sp_ref[:, lanes]
        a = jnp.exp(log_a)
        a_s[:, lanes] = a
        b_s[:, lanes] = jnp.sqrt(1.0 - a * a) * (i * xh)

    row = lax.broadcasted_iota(jnp.int32, (SUBLANES, LRU_W), 0)

    def group(ig, h):
        r0 = pl.multiple_of(ig * SUBLANES, SUBLANES)
        a = a_s[pl.ds(r0, SUBLANES), :]
        b = b_s[pl.ds(r0, SUBLANES), :]
        for sh in (1, 2, 4):
            keep = row >= sh
            a_sh = jnp.where(keep, pltpu.roll(a, sh, axis=0), 1.0)
            b_sh = jnp.where(keep, pltpu.roll(b, sh, axis=0), 0.0)
            b = a * b_sh + b
            a = a * a_sh
        hs = a * h + b
        o_ref[0, pl.ds(r0, SUBLANES), :] = (hs * gy_ref[0, pl.ds(r0, SUBLANES), :]).astype(o_ref.dtype)
        return hs[SUBLANES - 1:SUBLANES, :]

    h = lax.fori_loop(0, rows // SUBLANES, group, h_s[...])
    h_s[...] = h
    hl_ref[0] = h


def _rglru(xc, gy, w_ga, b_ga, w_gx, b_gx, sp, h0, *, rows):
    b, t, w = xc.shape
    assert t % rows == 0 and rows % SUBLANES == 0
    x_spec = pl.BlockSpec((1, rows, w), lambda ib, it: (ib, it, 0))
    full3 = lambda shape: pl.BlockSpec(shape, lambda ib, it: (0, 0, 0))
    h_spec = pl.BlockSpec((1, 1, w), lambda ib, it: (ib, 0, 0))
    vmem = 2 * (2 * rows * w * 4 + rows * w * 2 + 4 * LRU_HEADS * LRU_BW * LRU_BW * 4) + 2 * rows * w * 4 + (8 << 20)
    return pl.pallas_call(
        functools.partial(_lru_kernel, rows=rows),
        grid=(b, t // rows),
        in_specs=[x_spec, x_spec,
                  full3((LRU_HEADS, LRU_BW, LRU_BW)), full3((LRU_HEADS, 1, LRU_BW)),
                  full3((LRU_HEADS, LRU_BW, LRU_BW)), full3((LRU_HEADS, 1, LRU_BW)),
                  pl.BlockSpec((1, w), lambda ib, it: (0, 0)), h_spec],
        out_specs=[x_spec, h_spec],
        out_shape=[jax.ShapeDtypeStruct((b, t, w), BF16), jax.ShapeDtypeStruct((b, 1, w), F32)],
        scratch_shapes=[pltpu.VMEM((rows, w), F32), pltpu.VMEM((rows, w), F32), pltpu.VMEM((1, w), F32)],
        compiler_params=_params(2, vmem),
        name="rg_lru",
    )(xc, gy, w_ga, b_ga, w_gx, b_gx, sp, h0)


def _merge_kernel(oa_ref, ob_ref, wa_ref, wb_ref, ga_ref, gb_ref, o_ref):
    ya = jnp.dot(oa_ref[...], wa_ref[...].astype(BF16), preferred_element_type=F32)
    yb = jnp.dot(ob_ref[...], wb_ref[...].astype(BF16), preferred_element_type=F32)
    o_ref[...] = (jax.nn.sigmoid(ga_ref[...]) * ya + jax.nn.sigmoid(gb_ref[...]) * yb).astype(o_ref.dtype)


def _merge(oa, ob, wa, wb, gates, ga_col0, gb_col0, *, tm, tn):
    m = oa.shape[0]
    n = wa.shape[1]
    assert m % tm == 0 and n % tn == 0 and ga_col0 % tn == 0 and gb_col0 % tn == 0
    ka, kb = oa.shape[1], ob.shape[1]
    vmem = 2 * (tm * (ka + kb) * 2 + (ka + kb) * tn * 4 + 2 * tm * tn * 4 + tm * tn * 2) + (ka + kb) * tn * 2
    row = lambda kdim: pl.BlockSpec((tm, kdim), lambda i, j: (i, 0))
    col = lambda kdim: pl.BlockSpec((kdim, tn), lambda i, j: (0, j))
    gate = lambda col0: pl.BlockSpec((tm, tn), lambda i, j: (i, col0 // tn + j))
    return pl.pallas_call(
        _merge_kernel,
        grid=(m // tm, n // tn),
        in_specs=[row(ka), row(kb), col(ka), col(kb), gate(ga_col0), gate(gb_col0)],
        out_specs=pl.BlockSpec((tm, tn), lambda i, j: (i, j)),
        out_shape=jax.ShapeDtypeStruct((m, n), BF16),
        compiler_params=_params(2, vmem),
        name="gated_merge",
    )(oa, ob, wa, wb, gates, gates)


def _layer_norm(x, g, b):
    mu = jnp.mean(x, axis=-1, keepdims=True)
    xc = x - mu
    var = jnp.mean(xc * xc, axis=-1, keepdims=True)
    return xc * lax.rsqrt(var + LN_EPS) * g + b


def _out_kernel(m_ref, x_ref, w_ref, g_ref, b_ref, wr_ref, br_ref, h_ref, lg_ref):
    y = jnp.dot(m_ref[...], w_ref[...].astype(BF16), preferred_element_type=F32)
    h = _layer_norm(DEEPNORM_ALPHA * x_ref[...] + y, g_ref[...], b_ref[...])
    h_ref[...] = h
    lg_ref[...] = jnp.dot(h, wr_ref[...], preferred_element_type=F32,
                          precision=lax.Precision.HIGHEST) + br_ref[...]


def _out_proj(merged, x, w_out, g, b, w_router, b_router, *, tm):
    m, d = x.shape
    e = w_router.shape[1]
    assert m % tm == 0
    vmem = 2 * (tm * d * 2 + tm * d * 4 + d * d * 4 + tm * d * 4 + d * e * 4 + tm * 128 * 4) + d * d * 2
    rows = lambda width: pl.BlockSpec((tm, width), lambda i: (i, 0))
    full = lambda shape: pl.BlockSpec(shape, lambda i: (0, 0))
    return pl.pallas_call(
        _out_kernel,
        grid=(m // tm,),
        in_specs=[rows(d), rows(d), full((d, d)), full((1, d)), full((1, d)), full((d, e)), full((1, e))],
        out_specs=[rows(d), rows(e)],
        out_shape=[jax.ShapeDtypeStruct((m, d), F32), jax.ShapeDtypeStruct((m, e), F32)],
        compiler_params=_params(1, vmem),
        name="out_proj_ln_router",
    )(merged, x, w_out, g, b, w_router, b_router)


def _moe_kernel(ge_ref, gb_ref, gn_ref, x_ref, wg_ref, wu_ref, bg_ref, bu_ref, wd_ref, bd_ref, o_ref,
                wg_s, wu_s, wd_s):
    ig = pl.program_id(0)
    jf = pl.program_id(1)
    n_valid = gn_ref[ig]

    @pl.when(n_valid > 0)
    def _():
        @pl.when(jf == 0)
        def _():
            o_ref[...] = jnp.broadcast_to(bd_ref[0], o_ref.shape)

        wg_s[...] = wg_ref[0].astype(BF16)
        wu_s[...] = wu_ref[0].astype(BF16)
        wd_s[...] = wd_ref[0].astype(BF16)
        for sb in range(MOE_GROUP_ROWS // MOE_SUB_ROWS):
            @pl.when(sb * MOE_SUB_ROWS < n_valid)
            def _():
                rows = slice(sb * MOE_SUB_ROWS, (sb + 1) * MOE_SUB_ROWS)
                xb = x_ref[rows, :]
                hg = jnp.dot(xb, wg_s[...], preferred_element_type=F32) + bg_ref[0]
                hu = jnp.dot(xb, wu_s[...], preferred_element_type=F32) + bu_ref[0]
                hg = jnp.minimum(hg, SWIGLU_LIMIT)
                hu = jnp.clip(hu, -SWIGLU_LIMIT, SWIGLU_LIMIT)
                act = (hu + 1.0) * hg * jax.nn.sigmoid(SWIGLU_ALPHA * hg)
                o_ref[rows, :] += jnp.dot(act.astype(BF16), wd_s[...], preferred_element_type=F32)


def _moe_experts(x_rows, group_expert, group_block, group_valid, w_gu, b_gu, w_dn, b_dn):
    n_groups = group_expert.shape[0]
    d = x_rows.shape[1]
    nf = D_FF // MOE_FF_TILE
    gm, tf = MOE_GROUP_ROWS, MOE_FF_TILE

    def f_eff(ig, jf, gn):
        return jnp.where(gn[ig] > 0, jf, nf - 1)

    grid_spec = pltpu.PrefetchScalarGridSpec(
        num_scalar_prefetch=3,
        grid=(n_groups, nf),
        in_specs=[
            pl.BlockSpec((gm, d), lambda ig, jf, ge, gb, gn: (gb[ig], 0)),
            pl.BlockSpec((1, d, tf), lambda ig, jf, ge, gb, gn: (ge[ig], 0, f_eff(ig, jf, gn))),
            pl.BlockSpec((1, d, tf), lambda ig, jf, ge, gb, gn: (ge[ig], 0, nf + f_eff(ig, jf, gn))),
            pl.BlockSpec((1, 1, tf), lambda ig, jf, ge, gb, gn: (ge[ig], 0, f_eff(ig, jf, gn))),
            pl.BlockSpec((1, 1, tf), lambda ig, jf, ge, gb, gn: (ge[ig], 0, nf + f_eff(ig, jf, gn))),
            pl.BlockSpec((1, tf, d), lambda ig, jf, ge, gb, gn: (ge[ig], f_eff(ig, jf, gn), 0)),
            pl.BlockSpec((1, 1, d), lambda ig, jf, ge, gb, gn: (ge[ig], 0, 0)),
        ],
        out_specs=pl.BlockSpec((gm, d), lambda ig, jf, ge, gb, gn: (gb[ig], 0)),
        scratch_shapes=[pltpu.VMEM((d, tf), BF16), pltpu.VMEM((d, tf), BF16), pltpu.VMEM((tf, d), BF16)],
    )
    vmem = 2 * (gm * d * 2 + 3 * d * tf * 4 + gm * d * 4) + 3 * d * tf * 2 + (6 << 20)
    return pl.pallas_call(
        _moe_kernel,
        grid_spec=grid_spec,
        out_shape=jax.ShapeDtypeStruct((n_groups * gm, d), F32),
        compiler_params=_params(2, vmem),
        name="moe_experts",
    )(group_expert, group_block, group_valid, x_rows, w_gu, w_gu, b_gu, b_gu, w_dn, b_dn)


def _moe_ffn(h, logits, w_gu, b_gu, w_dn, b_dn):
    n, d = h.shape
    nk = n * TOP_K
    gm = MOE_GROUP_ROWS
    n_groups = N_EXPERTS + nk // gm
    top_logit, top_e = lax.top_k(logits, TOP_K)
    gate = jax.nn.softmax(top_logit, axis=-1)
    flat_e = top_e.reshape(nk)
    onehot = (flat_e[:, None] == jnp.arange(N_EXPERTS, dtype=flat_e.dtype)[None, :]).astype(jnp.int32)
    rank = jnp.take_along_axis(jnp.cumsum(onehot, axis=0), flat_e[:, None], axis=1)[:, 0] - 1
    counts = jnp.sum(onehot, axis=0)
    groups_e = (counts + gm - 1) // gm
    gend_e = jnp.cumsum(groups_e)
    gstart_e = gend_e - groups_e
    dest = gstart_e[flat_e] * gm + rank
    src_tok = jnp.zeros((n_groups * gm,), jnp.int32).at[dest].set(jnp.arange(nk, dtype=jnp.int32) // TOP_K)
    x_rows = h.astype(BF16)[src_tok]
    n_used = gend_e[-1]
    gidx = jnp.arange(n_groups, dtype=jnp.int32)
    gclamp = jnp.minimum(gidx, n_used - 1)
    group_expert = jnp.minimum(jnp.searchsorted(gend_e, gclamp, side="right"), N_EXPERTS - 1).astype(jnp.int32)
    within = gclamp - gstart_e[group_expert]
    group_valid = jnp.where(gidx < n_used, jnp.clip(counts[group_expert] - within * gm, 0, gm), 0).astype(jnp.int32)
    out_rows = _moe_experts(x_rows, group_expert, gclamp.astype(jnp.int32), group_valid,
                            w_gu, b_gu.reshape(N_EXPERTS, 1, -1), w_dn, b_dn.reshape(N_EXPERTS, 1, -1))
    picked = out_rows[dest.reshape(n, TOP_K)]
    return jnp.sum(picked * gate[:, :, None], axis=1)


def _causal_conv(x, prev, w, b):
    t = x.shape[1]
    xp = jnp.concatenate([prev, x], axis=1)
    y = sum(xp[:, j:j + t] * w[j] for j in range(CONV_W))
    if b is not None:
        y = y + b
    return y, xp[:, t:]


def _l2norm(u):
    return u * lax.rsqrt(jnp.sum(u * u, axis=-1, keepdims=True) + NORM_EPS)


def _mixers(p_a, p_ba, p_r, dn_conv, dn_s, lru_conv, lru_h, prm, *, pad_to, gdn_chunks, lru_rows):
    bsz, t, _ = p_a.shape
    qkv, dn_conv_new = _causal_conv(p_a[..., :OFF_Z], dn_conv, prm["w_dn_conv"], None)
    qkv = jax.nn.silu(qkv)
    q = _l2norm(qkv[..., :DN_QK_DIM].reshape(bsz, t, DN_QK_HEADS, DN_HEAD)) * (DN_HEAD ** -0.5)
    k = _l2norm(qkv[..., DN_QK_DIM:2 * DN_QK_DIM].reshape(bsz, t, DN_QK_HEADS, DN_HEAD))
    q = q.reshape(bsz, t, DN_QK_DIM)
    k = k.reshape(bsz, t, DN_QK_DIM)
    v = qkv[..., 2 * DN_QK_DIM:]
    zg = jax.nn.silu(p_a[..., OFF_Z:])
    beta = jax.nn.sigmoid(p_ba[..., :DN_V_HEADS])
    g = -jnp.exp(prm["dn_a_log"]) * jax.nn.softplus(p_ba[..., DN_V_HEADS:] + prm["dn_dt_bias"])
    tp = t if pad_to is None else pad_to
    if tp != t:
        padt = lambda u: jnp.pad(u, ((0, 0), (0, tp - t), (0, 0)))
        q, k, v, zg, beta, g = (padt(u) for u in (q, k, v, zg, beta, g))
    n_ch = tp // GDN_CHUNK
    to5 = lambda u: jnp.transpose(u, (0, 2, 1)).reshape(bsz, DN_QK_HEADS, 2, n_ch, GDN_CHUNK)
    o_a, dn_s_new = _gated_delta(q, k, v, zg, to5(g), to5(beta), dn_s, prm["dn_norm_w"].reshape(1, DN_HEAD),
                                 chunk=GDN_CHUNK, n_chunks=gdn_chunks)
    o_a = o_a[:, :t]

    xc, lru_conv_new = _causal_conv(p_r[..., :LRU_W], lru_conv, prm["w_lru_conv"], prm["b_lru_conv"])
    gy = jax.nn.gelu(p_r[..., LRU_W:2 * LRU_W])
    o_b, h_new = _rglru(xc, gy, prm["w_lru_ga"], prm["b_lru_ga"].reshape(LRU_HEADS, 1, LRU_BW),
                        prm["w_lru_gx"], prm["b_lru_gx"].reshape(LRU_HEADS, 1, LRU_BW),
                        jax.nn.softplus(-prm["lru_lambda"]).reshape(1, LRU_W),
                        lru_h.reshape(bsz, 1, LRU_W), rows=lru_rows)
    return o_a, o_b, dn_conv_new, dn_s_new, lru_conv_new, h_new.reshape(bsz, LRU_W)


def kernel(x_prompt, x_sample, state_dn_conv, state_dn_ssm, state_lru_conv, state_lru_h, w_in, w_dn_conv, dn_a_log, dn_dt_bias, dn_norm_w, w_lru_conv, b_lru_conv, w_lru_ga, b_lru_ga, w_lru_gx, b_lru_gx, lru_lambda, w_proj_a, w_proj_b, w_out, ln1_g, ln1_b, w_router, b_router, w_gate_up, b_gate_up, w_down, b_down, ln2_g, ln2_b):
    assert w_in.shape[0] == 1, "single layer"
    bp, tp, d = x_prompt.shape
    bs, ts, _ = x_sample.shape
    np_, ns = bp * tp, bs * ts
    n = np_ + ns
    prm = dict(w_dn_conv=w_dn_conv[0], dn_a_log=dn_a_log[0], dn_dt_bias=dn_dt_bias[0], dn_norm_w=dn_norm_w[0],
               w_lru_conv=w_lru_conv[0], b_lru_conv=b_lru_conv[0], w_lru_ga=w_lru_ga[0], b_lru_ga=b_lru_ga[0],
               w_lru_gx=w_lru_gx[0], b_lru_gx=b_lru_gx[0], lru_lambda=lru_lambda[0])

    x_all = jnp.concatenate([x_prompt.reshape(np_, d), x_sample.reshape(ns, d)], axis=0)
    x_bf = x_all.astype(BF16)
    w_in0 = w_in[0]
    tm = n // 5
    p_a = _matmul(x_bf, w_in0, tm=tm, tn=1024, n_cols=OFF_B)
    p_ba = _matmul(x_bf, w_in0[:, OFF_B:OFF_LX], tm=tm, tn=OFF_LX - OFF_B)
    p_r = _matmul(x_bf, w_in0[:, OFF_LX:], tm=tm, tn=1024)

    split = lambda u, width: (u[:np_].reshape(bp, tp, width), u[np_:].reshape(bs, ts, width))
    pa_p, pa_s = split(p_a, OFF_B)
    pba_p, pba_s = split(p_ba, OFF_LX - OFF_B)
    pr_p, pr_s = split(p_r[:, :2 * LRU_W], 2 * LRU_W)
    zeros = lambda *shape: jnp.zeros(shape, F32)
    res_p = _mixers(pa_p, pba_p, pr_p, zeros(bp, CONV_W - 1, DN_QKV_DIM), zeros(bp, DN_V_HEADS, DN_HEAD, DN_HEAD),
                    zeros(bp, CONV_W - 1, LRU_W), zeros(bp, LRU_W), prm,
                    pad_to=None, gdn_chunks=GDN_CHUNKS_PER_STEP, lru_rows=256)
    res_s = _mixers(pa_s, pba_s, pr_s, state_dn_conv[0], state_dn_ssm[0], state_lru_conv[0], state_lru_h[0], prm,
                    pad_to=GDN_CHUNK, gdn_chunks=1, lru_rows=ts)

    o_a = jnp.concatenate([res_p[0].reshape(np_, DN_V_DIM), res_s[0].reshape(ns, DN_V_DIM)], axis=0)
    o_b = jnp.concatenate([res_p[1].reshape(np_, LRU_W), res_s[1].reshape(ns, LRU_W)], axis=0)
    merged = _merge(o_a, o_b, w_proj_a[0], w_proj_b[0], p_r, 2 * LRU_W, 3 * LRU_W, tm=n // 10, tn=256)
    h, logits = _out_proj(merged, x_all, w_out[0], ln1_g, ln1_b, w_router[0], b_router, tm=n // 26)
    y = _moe_ffn(h, logits, w_gate_up[0], b_gate_up[0], w_down[0], b_down[0])
    out = _layer_norm(DEEPNORM_ALPHA * h + y, ln2_g[0], ln2_b[0])

    y_p = out[:np_].reshape(bp, tp, d)
    y_s = out[np_:].reshape(bs, ts, d)
    return (y_p, y_s,
            res_p[2][None], res_p[3][None], res_p[4][None], res_p[5][None],
            res_s[2][None], res_s[3][None], res_s[4][None], res_s[5][None])
```

```python
import functools

import jax
import jax.numpy as jnp
from jax import lax
from jax.experimental import pallas as pl
from jax.experimental.pallas import tpu as pltpu

F32 = jnp.float32
BF16 = jnp.bfloat16

D_MODEL = 2048
DN_QK_HEADS = 16
DN_V_HEADS = 32
DN_HEAD = 128
DN_QK_DIM = DN_QK_HEADS * DN_HEAD
DN_V_DIM = DN_V_HEADS * DN_HEAD
DN_QKV_DIM = 2 * DN_QK_DIM + DN_V_DIM
CONV_W = 4
LRU_W = D_MODEL
LRU_HEADS = 16
LRU_BW = LRU_W // LRU_HEADS
LRU_C = 8.0
N_EXPERTS = 32
TOP_K = 4
D_FF = D_MODEL
SWIGLU_LIMIT = 7.0
SWIGLU_ALPHA = 1.702
DEEPNORM_ALPHA = 2.0 ** 0.25
LN_EPS = 1e-5
NORM_EPS = 1e-6
OFF_Z = DN_QKV_DIM
OFF_B = OFF_Z + DN_V_DIM
OFF_A = OFF_B + DN_V_HEADS
OFF_LX = OFF_A + DN_V_HEADS
OFF_LY = OFF_LX + LRU_W
OFF_GA = OFF_LY + LRU_W
OFF_GB = OFF_GA + D_MODEL

V7X_VMEM_LIMIT_BYTES = 56 * 1024 * 1024
SUBLANES = 8

HALO = SUBLANES
GDN_CHUNK = 64
GDN_CHUNKS_PER_STEP = 4
GDN_QK_HEADS_PER_STEP = 4
GDN_WAVE_CHAINS = 16
MOE_GROUP_ROWS = 1280
MOE_SUB_ROWS = 256
MOE_FF_TILE = 256
COMBINE_ROWS = 128


def _params(n_axes, vmem_bytes):
    limit = min(V7X_VMEM_LIMIT_BYTES, int(vmem_bytes * 1.2) + (4 << 20))
    return pltpu.CompilerParams(dimension_semantics=("arbitrary",) * n_axes, vmem_limit_bytes=limit)


def _layer_norm(x, g, b):
    mu = jnp.mean(x, axis=-1, keepdims=True)
    xc = x - mu
    var = jnp.mean(xc * xc, axis=-1, keepdims=True)
    return xc * lax.rsqrt(var + LN_EPS) * g + b


def _causal_conv_rows(tail, x, w):
    r = x.shape[0]
    xx = jnp.concatenate([tail, x], axis=0)
    first = HALO - (CONV_W - 1)
    return sum(xx[first + j:first + j + r] * w[j:j + 1] for j in range(CONV_W))


def _mm_kernel(x_ref, w_ref, o_ref):
    o_ref[...] = jnp.dot(x_ref[...], w_ref[...].astype(BF16),
                         preferred_element_type=F32).astype(o_ref.dtype)


def _matmul(x_bf, w, *, tm, tn, n_cols=None):
    m, k = x_bf.shape
    n = w.shape[1] if n_cols is None else n_cols
    assert m % tm == 0 and n % tn == 0
    vmem = 2 * (tm * k * 2 + k * tn * 4 + tm * tn * 4) + k * tn * 2
    return pl.pallas_call(
        _mm_kernel,
        grid=(m // tm, n // tn),
        in_specs=[pl.BlockSpec((tm, k), lambda i, j: (i, 0)),
                  pl.BlockSpec((k, tn), lambda i, j: (0, j))],
        out_specs=pl.BlockSpec((tm, tn), lambda i, j: (i, j)),
        out_shape=jax.ShapeDtypeStruct((m, n), F32),
        compiler_params=_params(2, vmem),
        name="dense_matmul",
    )(x_bf, w)


def _dot_nt(a, b):
    return lax.dot_general(a, b, (((1,), (1,)), ((), ())), preferred_element_type=F32)


def _dot_tn(a, b):
    return lax.dot_general(a, b, (((0,), (0,)), ((), ())), preferred_element_type=F32)


def _l2norm(u):
    return u * lax.rsqrt(jnp.sum(u * u, axis=-1, keepdims=True) + NORM_EPS)


def _gdn_kernel(q_ref, k_ref, v_ref, z_ref, qp_ref, kp_ref, vp_ref, wq_ref, wk_ref, wv_ref,
                g_ref, beta_ref, s0_ref, nw_ref, *rest, chunk, n_chunks, n_qk):
    o_ref, s_ref, hist_ref = rest[-3:]
    c = chunk

    @pl.when(pl.program_id(2) == 0)
    def _():
        s_ref[...] = s0_ref[...]

    cw = 2 * c
    ri = lax.broadcasted_iota(jnp.int32, (c, cw), 0)
    ci = lax.broadcasted_iota(jnp.int32, (c, cw), 1)
    as_f = lambda m: jnp.where(m, 1.0, 0.0).astype(F32)
    causal_f, strict_f, eye_f = as_f(ri >= ci), as_f(ri > ci), as_f(ri == ci)
    le_f, hi_f, eye_hi = as_f(ri <= ci), as_f(ci >= c), as_f(ci == ri + c)
    n_square = c.bit_length() - 1
    nw = nw_ref[...]
    valid = min(c, q_ref.shape[0])
    zeros_c = lambda width: jnp.zeros((c, width), F32)

    def col_of(row):
        return jnp.sum(row * eye_f, axis=1, keepdims=True)

    def conv_silu(x_ref, p_ref, w_ref, ic, lanes, hist_col0):
        lo = ic * c
        if ic == 0:
            cols = slice(hist_col0 + lanes.start, hist_col0 + lanes.stop)
            hist_ref[0:HALO, cols] = p_ref[0, 0, :, lanes]
            hist_ref[HALO:HALO + valid, cols] = x_ref[0:valid, lanes]
            src, first, src_lanes = hist_ref, HALO - (CONV_W - 1), cols
        else:
            src, first, src_lanes = x_ref, lo - (CONV_W - 1), lanes
        y = sum(src[first + j:first + j + valid, src_lanes] * w_ref[j:j + 1, lanes] for j in range(CONV_W))
        y = y / (1.0 + jnp.exp(-y))
        return y if valid == c else jnp.concatenate([y, jnp.zeros((c - valid, y.shape[1]), F32)], axis=0)

    heads = range(2 * n_qk)
    wq = n_qk * DN_HEAD
    pre = {}
    wave_chunks = max(1, GDN_WAVE_CHAINS // (2 * n_qk))
    for w0 in range(0, n_chunks, wave_chunks):
        ics = range(w0, min(w0 + wave_chunks, n_chunks))
        qkeys = [(ic, j) for ic in ics for j in range(n_qk)]
        keys = [(ic, hv) for ic in ics for hv in heads]
        q, k = {}, {}
        for ic, j in qkeys:
            qk_lanes = slice(j * DN_HEAD, (j + 1) * DN_HEAD)
            q[ic, j] = _l2norm(conv_silu(q_ref, qp_ref, wq_ref, ic, qk_lanes, 0)) * (DN_HEAD ** -0.5)
            k[ic, j] = _l2norm(conv_silu(k_ref, kp_ref, wk_ref, ic, qk_lanes, wq))
        kq = {}
        for key in qkeys:
            k_t = jnp.concatenate([k[key], zeros_c(DN_HEAD)], axis=0).T.astype(BF16)
            kq[key] = jnp.dot(jnp.concatenate([k[key], q[key]], axis=0).astype(BF16), k_t,
                              preferred_element_type=F32)
        st, pt = {}, {}
        for ic, hv in keys:
            j, hh = divmod(hv, 2)
            chunk_row = pl.ds(pl.program_id(2) * n_chunks + ic, 1)
            g_row = g_ref[0, j, hh, chunk_row, :]
            beta_col = col_of(beta_ref[0, j, hh, chunk_row, :])
            gc_col = jnp.sum(g_row * causal_f, axis=1, keepdims=True)
            gc_row = jnp.sum(col_of(g_row) * le_f, axis=0, keepdims=True)
            decay = jnp.exp(jnp.minimum(gc_col - gc_row, 0.0)) * causal_f
            st[ic, hv] = dict(beta_col=beta_col, gc_col=gc_col, egc=jnp.exp(gc_col), g_last=gc_col[c - 1:c, :],
                              attn=(kq[ic, j][c:] * decay).astype(BF16))
            pt[ic, hv] = eye_hi - (beta_col * kq[ic, j][:c] * decay) * strict_f
        for _ in range(n_square):
            for key in keys:
                cur = pt[key]
                pt[key] = jnp.dot(cur.astype(BF16), jnp.concatenate([cur, zeros_c(cw)], axis=0).astype(BF16),
                                  preferred_element_type=F32) + cur * hi_f
        sol = {}
        for ic, hv in keys:
            a = st[ic, hv]
            v = conv_silu(v_ref, vp_ref, wv_ref, ic, slice(hv * DN_HEAD, (hv + 1) * DN_HEAD), 2 * wq)
            rhs = jnp.concatenate([v * a["beta_col"], k[ic, hv // 2] * (a["beta_col"] * a["egc"])], axis=1)
            rhs = jnp.concatenate([zeros_c(2 * DN_HEAD), rhs], axis=0)
            sol[ic, hv] = jnp.dot(pt[ic, hv].astype(BF16), rhs.astype(BF16), preferred_element_type=F32)
        for ic, hv in keys:
            a = st[ic, hv]
            kh = k[ic, hv // 2]
            pre[ic, hv] = dict(
                u_base=sol[ic, hv][:, :DN_HEAD],
                kc_qd=jnp.concatenate([sol[ic, hv][:, DN_HEAD:], q[ic, hv // 2] * a["egc"]], axis=0).astype(BF16),
                attn=a["attn"],
                k_dec_t=(kh * jnp.exp(a["g_last"] - a["gc_col"])).T.astype(BF16),
                s_decay=jnp.exp(a["g_last"]))

    s = {hv: s_ref[0, hv] for hv in heads}
    for ic in range(n_chunks):
        ks_qs = {hv: jnp.dot(pre[ic, hv]["kc_qd"], s[hv].astype(BF16), preferred_element_type=F32) for hv in heads}
        u = {hv: pre[ic, hv]["u_base"] - ks_qs[hv][:c] for hv in heads}
        o = {hv: ks_qs[hv][c:] + jnp.dot(pre[ic, hv]["attn"],
                                         jnp.concatenate([u[hv], zeros_c(DN_HEAD)], axis=0).astype(BF16),
                                         preferred_element_type=F32) for hv in heads}
        for hv in heads:
            s[hv] = s[hv] * pre[ic, hv]["s_decay"] + jnp.dot(pre[ic, hv]["k_dec_t"], u[hv].astype(BF16),
                                                           preferred_element_type=F32)
        for hv in heads:
            lanes = slice(hv * DN_HEAD, (hv + 1) * DN_HEAD)
            on = o[hv] * lax.rsqrt(jnp.mean(o[hv] * o[hv], axis=-1, keepdims=True) + NORM_EPS) * nw
            z = z_ref[ic * c:ic * c + valid, lanes]
            o_ref[ic * c:ic * c + valid, lanes] = (on[:valid] * (z * jax.nn.sigmoid(z))).astype(o_ref.dtype)
    for hv in heads:
        s_ref[0, hv] = s[hv]


def _gated_delta(p_a, row0, bsz, t, prev, w_conv, g5, beta5, s0, norm_w, o_init, *, chunk, n_chunks, n_qk):
    rows = min(t, chunk * n_chunks)
    nblk = t // rows
    assert t % rows == 0 and row0 % rows == 0 and (rows == chunk * n_chunks or n_chunks == 1)
    rb0 = row0 // rows
    wq, wv = DN_HEAD * n_qk, 2 * DN_HEAD * n_qk
    kq0, kv0, kz0 = DN_QK_DIM // wq, 2 * DN_QK_DIM // wv, OFF_Z // wv

    def rows_spec(width, col0):
        return pl.BlockSpec((rows, width), lambda ib, ih, it: (rb0 + ib * nblk + it, col0 + ih))

    def prev_spec(width, col0):
        return pl.BlockSpec((1, 1, HALO, width), lambda ib, ih, it: (ib, it, 0, col0 + ih))

    def w_spec(width, col0):
        return pl.BlockSpec((CONV_W, width), lambda ib, ih, it: (0, col0 + ih))

    g_spec = pl.BlockSpec((1, n_qk, 2) + g5.shape[3:], lambda ib, ih, it: (ib, ih, 0, 0, 0))
    s_spec = pl.BlockSpec((1, 2 * n_qk, DN_HEAD, DN_HEAD), lambda ib, ih, it: (ib, ih, 0, 0))
    in_specs = [rows_spec(wq, 0), rows_spec(wq, kq0), rows_spec(wv, kv0), rows_spec(wv, kz0),
                prev_spec(wq, 0), prev_spec(wq, kq0), prev_spec(wv, kv0),
                w_spec(wq, 0), w_spec(wq, kq0), w_spec(wv, kv0),
                g_spec, g_spec, s_spec, pl.BlockSpec((1, DN_HEAD), lambda ib, ih, it: (0, 0))]
    args = [p_a, p_a, p_a, p_a, prev, prev, prev, w_conv, w_conv, w_conv, g5, beta5, s0, norm_w]
    aliases = {}
    if o_init is not None:
        in_specs.append(pl.BlockSpec(memory_space=pl.ANY))
        args.append(o_init)
        aliases = {len(args) - 1: 0}
    vmem = 2 * (2 * rows * wq * 4 + 2 * rows * wv * 4 + rows * wv * 2 + 4 * n_qk * DN_HEAD * DN_HEAD * 4) + (16 << 20)
    return pl.pallas_call(
        functools.partial(_gdn_kernel, chunk=chunk, n_chunks=n_chunks, n_qk=n_qk),
        grid=(bsz, DN_QK_HEADS // n_qk, nblk),
        in_specs=in_specs,
        out_specs=[rows_spec(wv, 0), s_spec],
        out_shape=[jax.ShapeDtypeStruct((p_a.shape[0], DN_V_DIM), BF16),
                   jax.ShapeDtypeStruct((bsz, DN_V_HEADS, DN_HEAD, DN_HEAD), F32)],
        scratch_shapes=[pltpu.VMEM((HALO + min(chunk, rows), 2 * wq + wv), F32)],
        input_output_aliases=aliases,
        compiler_params=_params(3, vmem),
        name="gated_delta_rule",
    )(*args)


def _lru_kernel(x_ref, y_ref, xp_ref, wc_ref, bc_ref, wga_ref, bga_ref, wgx_ref, bgx_ref, sp_ref, h0_ref,
                *rest, rows):
    o_ref, hl_ref, a_s, b_s, h_s = rest[-5:]

    @pl.when(pl.program_id(1) == 0)
    def _():
        h_s[...] = h0_ref[0]

    for hd in range(LRU_HEADS):
        lanes = slice(hd * LRU_BW, (hd + 1) * LRU_BW)
        xh = _causal_conv_rows(xp_ref[0, 0, :, lanes], x_ref[:, lanes], wc_ref[:, lanes]) + bc_ref[:, lanes]
        xh_bf = xh.astype(BF16)
        r = jax.nn.sigmoid(jnp.dot(xh_bf, wga_ref[hd].astype(BF16), preferred_element_type=F32) + bga_ref[hd])
        i = jax.nn.sigmoid(jnp.dot(xh_bf, wgx_ref[hd].astype(BF16), preferred_element_type=F32) + bgx_ref[hd])
        log_a = (-LRU_C) * r * sp_ref[:, lanes]
        a = jnp.exp(log_a)
        a_s[:, lanes] = a
        b_s[:, lanes] = jnp.sqrt(1.0 - a * a) * (i * xh)

    row = lax.broadcasted_iota(jnp.int32, (SUBLANES, LRU_W), 0)

    def group(ig, h):
        r0 = pl.multiple_of(ig * SUBLANES, SUBLANES)
        a = a_s[pl.ds(r0, SUBLANES), :]
        b = b_s[pl.ds(r0, SUBLANES), :]
        for sh in (1, 2, 4):
            keep = row >= sh
            a_sh = jnp.where(keep, pltpu.roll(a, sh, axis=0), 1.0)
            b_sh = jnp.where(keep, pltpu.roll(b, sh, axis=0), 0.0)
            b = a * b_sh + b
            a = a * a_sh
        hs = a * h + b
        gy = jax.nn.gelu(y_ref[pl.ds(r0, SUBLANES), :])
        o_ref[pl.ds(r0, SUBLANES), :] = (hs * gy).astype(o_ref.dtype)
        return hs[SUBLANES - 1:SUBLANES, :]

    h = lax.fori_loop(0, rows // SUBLANES, group, h_s[...])
    h_s[...] = h
    hl_ref[0] = h


def _rglru(p_r, row0, bsz, t, prev, w_conv, b_conv, w_ga, b_ga, w_gx, b_gx, sp, h0, o_init, *, rows):
    w = LRU_W
    nblk = t // rows
    assert t % rows == 0 and rows % SUBLANES == 0 and row0 % rows == 0
    rb0 = row0 // rows
    rows_spec = lambda col: pl.BlockSpec((rows, w), lambda ib, it: (rb0 + ib * nblk + it, col))
    full3 = lambda shape: pl.BlockSpec(shape, lambda ib, it: (0, 0, 0))
    full2 = lambda shape: pl.BlockSpec(shape, lambda ib, it: (0, 0))
    h_spec = pl.BlockSpec((1, 1, w), lambda ib, it: (ib, 0, 0))
    in_specs = [rows_spec(0), rows_spec(1),
                pl.BlockSpec((1, 1, HALO, w), lambda ib, it: (ib, it, 0, 0)),
                full2((CONV_W, w)), full2((1, w)),
                full3((LRU_HEADS, LRU_BW, LRU_BW)), full3((LRU_HEADS, 1, LRU_BW)),
                full3((LRU_HEADS, LRU_BW, LRU_BW)), full3((LRU_HEADS, 1, LRU_BW)),
                full2((1, w)), h_spec]
    args = [p_r, p_r, prev, w_conv, b_conv, w_ga, b_ga, w_gx, b_gx, sp, h0]
    aliases = {}
    if o_init is not None:
        in_specs.append(pl.BlockSpec(memory_space=pl.ANY))
        args.append(o_init)
        aliases = {len(args) - 1: 0}
    vmem = 2 * (2 * rows * w * 4 + rows * w * 2 + 4 * LRU_HEADS * LRU_BW * LRU_BW * 4) + 2 * rows * w * 4 + (8 << 20)
    return pl.pallas_call(
        functools.partial(_lru_kernel, rows=rows),
        grid=(bsz, nblk),
        in_specs=in_specs,
        out_specs=[rows_spec(0), h_spec],
        out_shape=[jax.ShapeDtypeStruct((p_r.shape[0], w), BF16), jax.ShapeDtypeStruct((bsz, 1, w), F32)],
        scratch_shapes=[pltpu.VMEM((rows, w), F32), pltpu.VMEM((rows, w), F32), pltpu.VMEM((1, w), F32)],
        input_output_aliases=aliases,
        compiler_params=_params(2, vmem),
        name="rg_lru",
    )(*args)


def _merge_kernel(oa_ref, ob_ref, wa_ref, wb_ref, ga_ref, gb_ref, o_ref):
    ya = jnp.dot(oa_ref[...], wa_ref[...].astype(BF16), preferred_element_type=F32)
    yb = jnp.dot(ob_ref[...], wb_ref[...].astype(BF16), preferred_element_type=F32)
    o_ref[...] = (jax.nn.sigmoid(ga_ref[...]) * ya + jax.nn.sigmoid(gb_ref[...]) * yb).astype(o_ref.dtype)


def _merge(oa, ob, wa, wb, gates, ga_col0, gb_col0, *, tm, tn):
    m = oa.shape[0]
    n = wa.shape[1]
    assert m % tm == 0 and n % tn == 0 and ga_col0 % tn == 0 and gb_col0 % tn == 0
    ka, kb = oa.shape[1], ob.shape[1]
    vmem = 2 * (tm * (ka + kb) * 2 + (ka + kb) * tn * 4 + 2 * tm * tn * 4 + tm * tn * 2) + (ka + kb) * tn * 2
    row = lambda kdim: pl.BlockSpec((tm, kdim), lambda i, j: (i, 0))
    col = lambda kdim: pl.BlockSpec((kdim, tn), lambda i, j: (0, j))
    gate = lambda col0: pl.BlockSpec((tm, tn), lambda i, j: (i, col0 // tn + j))
    return pl.pallas_call(
        _merge_kernel,
        grid=(m // tm, n // tn),
        in_specs=[row(ka), row(kb), col(ka), col(kb), gate(ga_col0), gate(gb_col0)],
        out_specs=pl.BlockSpec((tm, tn), lambda i, j: (i, j)),
        out_shape=jax.ShapeDtypeStruct((m, n), BF16),
        compiler_params=_params(2, vmem),
        name="gated_merge",
    )(oa, ob, wa, wb, gates, gates)


def _out_kernel(m_ref, x_ref, w_ref, g_ref, b_ref, wr_ref, br_ref, h_ref, lg_ref):
    y = jnp.dot(m_ref[...], w_ref[...].astype(BF16), preferred_element_type=F32)
    h = _layer_norm(DEEPNORM_ALPHA * x_ref[...] + y, g_ref[...], b_ref[...])
    h_ref[...] = h
    lg_ref[...] = jnp.dot(h, wr_ref[...], preferred_element_type=F32,
                          precision=lax.Precision.HIGHEST) + br_ref[...]


def _out_proj(merged, x, w_out, g, b, w_router, b_router, *, tm):
    m, d = x.shape
    e = w_router.shape[1]
    assert m % tm == 0
    vmem = 2 * (tm * d * 2 + tm * d * 4 + d * d * 4 + tm * d * 4 + d * e * 4 + tm * 128 * 4) + d * d * 2
    rows = lambda width: pl.BlockSpec((tm, width), lambda i: (i, 0))
    full = lambda shape: pl.BlockSpec(shape, lambda i: (0, 0))
    return pl.pallas_call(
        _out_kernel,
        grid=(m // tm,),
        in_specs=[rows(d), rows(d), full((d, d)), full((1, d)), full((1, d)), full((d, e)), full((1, e))],
        out_specs=[rows(d), rows(e)],
        out_shape=[jax.ShapeDtypeStruct((m, d), F32), jax.ShapeDtypeStruct((m, e), F32)],
        compiler_params=_params(1, vmem),
        name="out_proj_ln_router",
    )(merged, x, w_out, g, b, w_router, b_router)


def _moe_kernel(ge_ref, gb_ref, gn_ref, x_ref, wg_ref, wu_ref, bg_ref, bu_ref, wd_ref, bd_ref, o_ref,
                wg_s, wu_s, wd_s):
    ig = pl.program_id(0)
    jf = pl.program_id(1)
    n_valid = gn_ref[ig]

    @pl.when((n_valid == 0) & (jf == 0))
    def _():
        o_ref[...] = jnp.zeros(o_ref.shape, o_ref.dtype)

    @pl.when(n_valid > 0)
    def _():
        @pl.when(jf == 0)
        def _():
            o_ref[...] = jnp.broadcast_to(bd_ref[0], o_ref.shape)

        wg_s[...] = wg_ref[0].astype(BF16)
        wu_s[...] = wu_ref[0].astype(BF16)
        wd_s[...] = wd_ref[0].astype(BF16)
        for sb in range(MOE_GROUP_ROWS // MOE_SUB_ROWS):
            @pl.when(sb * MOE_SUB_ROWS < n_valid)
            def _():
                rows = slice(sb * MOE_SUB_ROWS, (sb + 1) * MOE_SUB_ROWS)
                xb = x_ref[rows, :]
                hg = jnp.dot(xb, wg_s[...], preferred_element_type=F32) + bg_ref[0]
                hu = jnp.dot(xb, wu_s[...], preferred_element_type=F32) + bu_ref[0]
                hg = jnp.minimum(hg, SWIGLU_LIMIT)
                hu = jnp.clip(hu, -SWIGLU_LIMIT, SWIGLU_LIMIT)
                act = (hu + 1.0) * hg * jax.nn.sigmoid(SWIGLU_ALPHA * hg)
                o_ref[rows, :] += jnp.dot(act.astype(BF16), wd_s[...], preferred_element_type=F32)


def _moe_experts(x_rows, group_expert, group_block, group_valid, w_gu, b_gu, w_dn, b_dn):
    n_groups = group_expert.shape[0]
    d = x_rows.shape[1]
    nf = D_FF // MOE_FF_TILE
    gm, tf = MOE_GROUP_ROWS, MOE_FF_TILE

    def f_eff(ig, jf, gn):
        return jnp.where(gn[ig] > 0, jf, nf - 1)

    grid_spec = pltpu.PrefetchScalarGridSpec(
        num_scalar_prefetch=3,
        grid=(n_groups, nf),
        in_specs=[
            pl.BlockSpec((gm, d), lambda ig, jf, ge, gb, gn: (gb[ig], 0)),
            pl.BlockSpec((1, d, tf), lambda ig, jf, ge, gb, gn: (ge[ig], 0, f_eff(ig, jf, gn))),
            pl.BlockSpec((1, d, tf), lambda ig, jf, ge, gb, gn: (ge[ig], 0, nf + f_eff(ig, jf, gn))),
            pl.BlockSpec((1, 1, tf), lambda ig, jf, ge, gb, gn: (ge[ig], 0, f_eff(ig, jf, gn))),
            pl.BlockSpec((1, 1, tf), lambda ig, jf, ge, gb, gn: (ge[ig], 0, nf + f_eff(ig, jf, gn))),
            pl.BlockSpec((1, tf, d), lambda ig, jf, ge, gb, gn: (ge[ig], f_eff(ig, jf, gn), 0)),
            pl.BlockSpec((1, 1, d), lambda ig, jf, ge, gb, gn: (ge[ig], 0, 0)),
        ],
        out_specs=pl.BlockSpec((gm, d), lambda ig, jf, ge, gb, gn: (ig, 0)),
        scratch_shapes=[pltpu.VMEM((d, tf), BF16), pltpu.VMEM((d, tf), BF16), pltpu.VMEM((tf, d), BF16)],
    )
    vmem = 2 * (gm * d * 2 + 3 * d * tf * 4 + gm * d * 4) + 3 * d * tf * 2 + (6 << 20)
    return pl.pallas_call(
        _moe_kernel,
        grid_spec=grid_spec,
        out_shape=jax.ShapeDtypeStruct((n_groups * gm, d), F32),
        compiler_params=_params(2, vmem),
        name="moe_experts",
    )(group_expert, group_block, group_valid, x_rows, w_gu, w_gu, b_gu, b_gu, w_dn, b_dn)


def _combine_kernel(dest_ref, rows_hbm, gate_ref, h_ref, g_ref, b_ref, o_ref, buf, sem, *, tm):
    i = pl.program_id(0)
    n_steps = pl.num_programs(0)

    def row_copy(step, slot, t, kk):
        src = dest_ref[(step * tm + t) * TOP_K + kk]
        return pltpu.make_async_copy(rows_hbm.at[pl.ds(src, 1), :], buf.at[slot, pl.ds(kk * tm + t, 1), :],
                                     sem.at[slot])

    def for_all_rows(step, slot, fn):
        def body(t, carry):
            for kk in range(TOP_K):
                fn(row_copy(step, slot, t, kk))
            return carry
        lax.fori_loop(0, tm, body, 0)

    @pl.when(i == 0)
    def _():
        for_all_rows(0, 0, lambda cp: cp.start())

    @pl.when(i + 1 < n_steps)
    def _():
        for_all_rows(i + 1, (i + 1) % 2, lambda cp: cp.start())

    slot = i % 2
    for_all_rows(i, slot, lambda cp: cp.wait())
    y = sum(gate_ref[:, kk:kk + 1] * buf[slot, kk * tm:(kk + 1) * tm, :] for kk in range(TOP_K))
    o_ref[...] = _layer_norm(DEEPNORM_ALPHA * h_ref[...] + y, g_ref[...], b_ref[...])


def _combine(out_rows, dest, gate, h, g, b, *, tm):
    n, d = h.shape
    assert n % tm == 0
    grid_spec = pltpu.PrefetchScalarGridSpec(
        num_scalar_prefetch=1,
        grid=(n // tm,),
        in_specs=[pl.BlockSpec(memory_space=pl.ANY),
                  pl.BlockSpec((tm, TOP_K), lambda i, dest: (i, 0)),
                  pl.BlockSpec((tm, d), lambda i, dest: (i, 0)),
                  pl.BlockSpec((1, d), lambda i, dest: (0, 0)),
                  pl.BlockSpec((1, d), lambda i, dest: (0, 0))],
        out_specs=pl.BlockSpec((tm, d), lambda i, dest: (i, 0)),
        scratch_shapes=[pltpu.VMEM((2, TOP_K * tm, d), F32), pltpu.SemaphoreType.DMA((2,))],
    )
    vmem = 2 * TOP_K * tm * d * 4 + 2 * (2 * tm * d * 4 + tm * 128 * 4) + (4 << 20)
    return pl.pallas_call(
        functools.partial(_combine_kernel, tm=tm),
        grid_spec=grid_spec,
        out_shape=jax.ShapeDtypeStruct((n, d), F32),
        compiler_params=_params(1, vmem),
        name="moe_combine_ln",
    )(dest, out_rows, gate, h, g, b)


def _moe_ffn_ln(h, logits, w_gu, b_gu, w_dn, b_dn, ln_g, ln_b):
    n, d = h.shape
    nk = n * TOP_K
    gm = MOE_GROUP_ROWS
    n_groups = N_EXPERTS + nk // gm
    top_logit, top_e = lax.top_k(logits, TOP_K)
    gate = jax.nn.softmax(top_logit, axis=-1)
    flat_e = top_e.reshape(nk)
    onehot = (flat_e[:, None] == jnp.arange(N_EXPERTS, dtype=flat_e.dtype)[None, :]).astype(jnp.int32)
    rank = jnp.take_along_axis(jnp.cumsum(onehot, axis=0), flat_e[:, None], axis=1)[:, 0] - 1
    counts = jnp.sum(onehot, axis=0)
    groups_e = (counts + gm - 1) // gm
    gend_e = jnp.cumsum(groups_e)
    gstart_e = gend_e - groups_e
    dest = (gstart_e[flat_e] * gm + rank).astype(jnp.int32)
    src_tok = jnp.zeros((n_groups * gm,), jnp.int32).at[dest].set(jnp.arange(nk, dtype=jnp.int32) // TOP_K)
    x_rows = h.astype(BF16)[src_tok]
    n_used = gend_e[-1]
    gidx = jnp.arange(n_groups, dtype=jnp.int32)
    gclamp = jnp.minimum(gidx, n_used - 1)
    group_expert = jnp.minimum(jnp.searchsorted(gend_e, gclamp, side="right"), N_EXPERTS - 1).astype(jnp.int32)
    within = gclamp - gstart_e[group_expert]
    group_valid = jnp.where(gidx < n_used, jnp.clip(counts[group_expert] - within * gm, 0, gm), 0).astype(jnp.int32)
    out_rows = _moe_experts(x_rows, group_expert, gclamp.astype(jnp.int32), group_valid,
                            w_gu, b_gu.reshape(N_EXPERTS, 1, -1), w_dn, b_dn.reshape(N_EXPERTS, 1, -1))
    return _combine(out_rows, dest, gate, h, ln_g, ln_b, tm=COMBINE_ROWS)


def _history(p, width, row0, bsz, t, rows, state):
    first = jnp.pad(state, ((0, 0), (HALO - (CONV_W - 1), 0), (0, 0)))[:, None]
    nblk = t // rows
    if nblk == 1:
        return first
    body = p[row0:row0 + bsz * t].reshape(bsz, nblk, rows, p.shape[1])[:, :-1, rows - HALO:, :width]
    return jnp.concatenate([first, body], axis=1)


def _dn_gates(p_ba, row0, bsz, t, chunk, a_log, dt_bias):
    ba = p_ba[row0:row0 + bsz * t].reshape(bsz, t, 2 * DN_V_HEADS)
    beta = jax.nn.sigmoid(ba[..., :DN_V_HEADS])
    g = -jnp.exp(a_log) * jax.nn.softplus(ba[..., DN_V_HEADS:] + dt_bias)
    n_ch = -(-t // chunk)

    def to5(u):
        u = jnp.pad(u, ((0, 0), (0, n_ch * chunk - t), (0, 0)))
        u = jnp.transpose(u, (0, 2, 1)).reshape(bsz, DN_QK_HEADS, 2, n_ch, chunk)
        return jnp.pad(u, ((0, 0),) * 4 + ((0, chunk),))

    return to5(g), to5(beta)


def kernel(x_prompt, x_sample, state_dn_conv, state_dn_ssm, state_lru_conv, state_lru_h, w_in, w_dn_conv, dn_a_log, dn_dt_bias, dn_norm_w, w_lru_conv, b_lru_conv, w_lru_ga, b_lru_ga, w_lru_gx, b_lru_gx, lru_lambda, w_proj_a, w_proj_b, w_out, ln1_g, ln1_b, w_router, b_router, w_gate_up, b_gate_up, w_down, b_down, ln2_g, ln2_b):
    assert w_in.shape[0] == 1, "single layer"
    bp, tp, d = x_prompt.shape
    bs, ts, _ = x_sample.shape
    np_, ns = bp * tp, bs * ts
    n = np_ + ns

    x_all = jnp.concatenate([x_prompt.reshape(np_, d), x_sample.reshape(ns, d)], axis=0)
    x_bf = x_all.astype(BF16)
    w_in0 = w_in[0]
    tm = n // 5
    p_a = _matmul(x_bf, w_in0, tm=tm, tn=1024, n_cols=OFF_B)
    p_ba = _matmul(x_bf, w_in0[:, OFF_B:OFF_LX], tm=tm, tn=OFF_LX - OFF_B)
    p_r = _matmul(x_bf, w_in0[:, OFF_LX:], tm=tm, tn=1024)

    zeros = lambda *shape: jnp.zeros(shape, F32)
    norm_w = dn_norm_w[0].reshape(1, DN_HEAD)
    gdn_rows_p = GDN_CHUNK * GDN_CHUNKS_PER_STEP
    g5, beta5 = _dn_gates(p_ba, 0, bp, tp, GDN_CHUNK, dn_a_log[0], dn_dt_bias[0])
    o_a, dn_s_p = _gated_delta(
        p_a, 0, bp, tp, _history(p_a, DN_QKV_DIM, 0, bp, tp, gdn_rows_p, zeros(bp, CONV_W - 1, DN_QKV_DIM)),
        w_dn_conv[0], g5, beta5, zeros(bp, DN_V_HEADS, DN_HEAD, DN_HEAD), norm_w, jnp.zeros((n, DN_V_DIM), BF16),
        chunk=GDN_CHUNK, n_chunks=GDN_CHUNKS_PER_STEP, n_qk=GDN_QK_HEADS_PER_STEP)
    g5, beta5 = _dn_gates(p_ba, np_, bs, ts, GDN_CHUNK, dn_a_log[0], dn_dt_bias[0])
    o_a, dn_s_s = _gated_delta(
        p_a, np_, bs, ts, _history(p_a, DN_QKV_DIM, np_, bs, ts, ts, state_dn_conv[0]),
        w_dn_conv[0], g5, beta5, state_dn_ssm[0], norm_w, o_a, chunk=GDN_CHUNK, n_chunks=1, n_qk=4)

    lru_args = (w_lru_conv[0], b_lru_conv, w_lru_ga[0], b_lru_ga[0].reshape(LRU_HEADS, 1, LRU_BW),
                w_lru_gx[0], b_lru_gx[0].reshape(LRU_HEADS, 1, LRU_BW),
                jax.nn.softplus(-lru_lambda[0]).reshape(1, LRU_W))
    lru_rows_p = 256
    o_b, h_p = _rglru(p_r, 0, bp, tp, _history(p_r, LRU_W, 0, bp, tp, lru_rows_p, zeros(bp, CONV_W - 1, LRU_W)),
                      *lru_args, zeros(bp, 1, LRU_W), jnp.zeros((n, LRU_W), BF16), rows=lru_rows_p)
    o_b, h_s = _rglru(p_r, np_, bs, ts, _history(p_r, LRU_W, np_, bs, ts, ts, state_lru_conv[0]),
                      *lru_args, state_lru_h[0].reshape(bs, 1, LRU_W), o_b, rows=ts)

    merged = _merge(o_a, o_b, w_proj_a[0], w_proj_b[0], p_r, 2 * LRU_W, 3 * LRU_W, tm=n // 10, tn=256)
    h, logits = _out_proj(merged, x_all, w_out[0], ln1_g, ln1_b, w_router[0], b_router, tm=n // 26)
    out = _moe_ffn_ln(h, logits, w_gate_up[0], b_gate_up[0], w_down[0], b_down[0], ln2_g, ln2_b)

    tail = lambda p, width, lo, b, t: p[lo:lo + b * t].reshape(b, t, -1)[:, t - (CONV_W - 1):, :width]
    return (out[:np_].reshape(bp, tp, d), out[np_:].reshape(bs, ts, d),
            tail(p_a, DN_QKV_DIM, 0, bp, tp)[None], dn_s_p[None],
            tail(p_r, LRU_W, 0, bp, tp)[None], h_p.reshape(bp, LRU_W)[None],
            tail(p_a, DN_QKV_DIM, np_, bs, ts)[None], dn_s_s[None],
            tail(p_r, LRU_W, np_, bs, ts)[None], h_s.reshape(bs, LRU_W)[None])
```

```python
import functools

import jax
import jax.numpy as jnp
from jax import lax
from jax.experimental import pallas as pl
from jax.experimental.pallas import tpu as pltpu

F32 = jnp.float32
BF16 = jnp.bfloat16

D_MODEL = 2048
DN_QK_HEADS = 16
DN_V_HEADS = 32
DN_HEAD = 128
DN_QK_DIM = DN_QK_HEADS * DN_HEAD
DN_V_DIM = DN_V_HEADS * DN_HEAD
DN_QKV_DIM = 2 * DN_QK_DIM + DN_V_DIM
CONV_W = 4
LRU_W = D_MODEL
LRU_HEADS = 16
LRU_BW = LRU_W // LRU_HEADS
LRU_C = 8.0
N_EXPERTS = 32
TOP_K = 4
D_FF = D_MODEL
SWIGLU_LIMIT = 7.0
SWIGLU_ALPHA = 1.702
DEEPNORM_ALPHA = 2.0 ** 0.25
LN_EPS = 1e-5
NORM_EPS = 1e-6
OFF_Z = DN_QKV_DIM
OFF_B = OFF_Z + DN_V_DIM
OFF_A = OFF_B + DN_V_HEADS
OFF_LX = OFF_A + DN_V_HEADS
OFF_LY = OFF_LX + LRU_W
OFF_GA = OFF_LY + LRU_W
OFF_GB = OFF_GA + D_MODEL

V7X_VMEM_LIMIT_BYTES = 56 * 1024 * 1024
SUBLANES = 8
LANES = 128

HALO = SUBLANES
GDN_CHUNK = 64
GDN_CHUNKS_PER_STEP = 4
GDN_QK_HEADS_PER_STEP = 4
GDN_WAVE_CHAINS = 16
MOE_GROUP_ROWS = 1280
MOE_SUB_ROWS = 256
MOE_FF_TILE = 256
COMBINE_ROWS = 128


def _params(n_axes, vmem_bytes):
    limit = min(V7X_VMEM_LIMIT_BYTES, int(vmem_bytes * 1.2) + (4 << 20))
    return pltpu.CompilerParams(dimension_semantics=("arbitrary",) * n_axes, vmem_limit_bytes=limit)


def _layer_norm(x, g, b):
    mu = jnp.mean(x, axis=-1, keepdims=True)
    xc = x - mu
    var = jnp.mean(xc * xc, axis=-1, keepdims=True)
    return xc * lax.rsqrt(var + LN_EPS) * g + b


def _causal_conv_rows(tail, x, w):
    r = x.shape[0]
    xx = jnp.concatenate([tail, x], axis=0)
    first = HALO - (CONV_W - 1)
    return sum(xx[first + j:first + j + r] * w[j:j + 1] for j in range(CONV_W))


def _mm_kernel(x_ref, w_ref, o_ref):
    o_ref[...] = jnp.dot(x_ref[...], w_ref[...].astype(BF16),
                         preferred_element_type=F32).astype(o_ref.dtype)


def _mm_shift_kernel(x_ref, wa_ref, wb_ref, o_ref, *, shift):
    w = jnp.concatenate([wa_ref[:, shift:], wb_ref[:, :shift]], axis=1)
    o_ref[...] = jnp.dot(x_ref[...], w.astype(BF16), preferred_element_type=F32).astype(o_ref.dtype)


def _matmul(x_bf, w, *, tm, tn, col0=0, n_cols=None):
    m, k = x_bf.shape
    n = w.shape[1] - col0 if n_cols is None else n_cols
    shift = col0 % tn
    assert m % tm == 0 and n % tn == 0 and shift % 64 == 0 and shift <= LANES
    vmem = 2 * (tm * k * 2 + k * (tn + LANES) * 4 + tm * tn * 4) + k * tn * 6
    x_spec = pl.BlockSpec((tm, k), lambda i, j: (i, 0))
    w_spec = pl.BlockSpec((k, tn), lambda i, j: (0, col0 // tn + j))
    if shift == 0:
        body, w_specs, w_args = _mm_kernel, [w_spec], [w]
    else:
        body = functools.partial(_mm_shift_kernel, shift=shift)
        w_specs = [w_spec, pl.BlockSpec((k, LANES), lambda i, j: (0, (col0 // tn + j + 1) * (tn // LANES)))]
        w_args = [w, w]
    return pl.pallas_call(
        body,
        grid=(m // tm, n // tn),
        in_specs=[x_spec] + w_specs,
        out_specs=pl.BlockSpec((tm, tn), lambda i, j: (i, j)),
        out_shape=jax.ShapeDtypeStruct((m, n), F32),
        compiler_params=_params(2, vmem),
        name="dense_matmul",
    )(x_bf, *w_args)


def _dot_nt(a, b):
    return lax.dot_general(a, b, (((1,), (1,)), ((), ())), preferred_element_type=F32)


def _dot_tn(a, b):
    return lax.dot_general(a, b, (((0,), (0,)), ((), ())), preferred_element_type=F32)


def _l2norm(u):
    return u * lax.rsqrt(jnp.sum(u * u, axis=-1, keepdims=True) + NORM_EPS)


def _gdn_kernel(q_ref, k_ref, v_ref, z_ref, qh_ref, kh_ref, vh_ref, qs_ref, ks_ref, vs_ref, wq_ref, wk_ref, wv_ref,
                g_ref, beta_ref, s0_ref, nw_ref, *rest, chunk, n_chunks, n_qk):
    o_ref, s_ref, hist_ref = rest[-3:]
    c = chunk

    @pl.when(pl.program_id(2) == 0)
    def _():
        s_ref[...] = s0_ref[...]

    cw = 2 * c
    ri = lax.broadcasted_iota(jnp.int32, (c, cw), 0)
    ci = lax.broadcasted_iota(jnp.int32, (c, cw), 1)
    as_f = lambda m: jnp.where(m, 1.0, 0.0).astype(F32)
    causal_f, strict_f, eye_f = as_f(ri >= ci), as_f(ri > ci), as_f(ri == ci)
    le_f, hi_f, eye_hi = as_f(ri <= ci), as_f(ci >= c), as_f(ci == ri + c)
    n_square = c.bit_length() - 1
    nw = nw_ref[...]
    valid = min(c, q_ref.shape[0])
    zeros_c = lambda width: jnp.zeros((c, width), F32)

    def col_of(row):
        return jnp.sum(row * eye_f, axis=1, keepdims=True)

    def conv_silu(x_ref, halo_ref, state_ref, w_ref, ic, lanes, hist_col0):
        lo = ic * c
        if ic == 0:
            cols = slice(hist_col0 + lanes.start, hist_col0 + lanes.stop)
            hist_ref[0:HALO, cols] = jnp.where(pl.program_id(2) == 0, state_ref[0, :, lanes], halo_ref[:, lanes])
            hist_ref[HALO:HALO + valid, cols] = x_ref[0:valid, lanes]
            src, first, src_lanes = hist_ref, HALO - (CONV_W - 1), cols
        else:
            src, first, src_lanes = x_ref, lo - (CONV_W - 1), lanes
        y = sum(src[first + j:first + j + valid, src_lanes] * w_ref[j:j + 1, lanes] for j in range(CONV_W))
        y = y / (1.0 + jnp.exp(-y))
        return y if valid == c else jnp.concatenate([y, jnp.zeros((c - valid, y.shape[1]), F32)], axis=0)

    heads = range(2 * n_qk)
    wq = n_qk * DN_HEAD
    pre = {}
    wave_chunks = max(1, GDN_WAVE_CHAINS // (2 * n_qk))
    for w0 in range(0, n_chunks, wave_chunks):
        ics = range(w0, min(w0 + wave_chunks, n_chunks))
        qkeys = [(ic, j) for ic in ics for j in range(n_qk)]
        keys = [(ic, hv) for ic in ics for hv in heads]
        q, k = {}, {}
        for ic, j in qkeys:
            qk_lanes = slice(j * DN_HEAD, (j + 1) * DN_HEAD)
            q[ic, j] = _l2norm(conv_silu(q_ref, qh_ref, qs_ref, wq_ref, ic, qk_lanes, 0)) * (DN_HEAD ** -0.5)
            k[ic, j] = _l2norm(conv_silu(k_ref, kh_ref, ks_ref, wk_ref, ic, qk_lanes, wq))
        kq = {}
        for key in qkeys:
            k_t = jnp.concatenate([k[key], zeros_c(DN_HEAD)], axis=0).T.astype(BF16)
            kq[key] = jnp.dot(jnp.concatenate([k[key], q[key]], axis=0).astype(BF16), k_t,
                              preferred_element_type=F32)
        st, pt = {}, {}
        for ic, hv in keys:
            j, hh = divmod(hv, 2)
            chunk_row = pl.ds(pl.program_id(2) * n_chunks + ic, 1)
            g_row = g_ref[0, j, hh, chunk_row, :]
            beta_col = col_of(beta_ref[0, j, hh, chunk_row, :])
            gc_col = jnp.sum(g_row * causal_f, axis=1, keepdims=True)
            gc_row = jnp.sum(col_of(g_row) * le_f, axis=0, keepdims=True)
            decay = jnp.exp(jnp.minimum(gc_col - gc_row, 0.0)) * causal_f
            st[ic, hv] = dict(beta_col=beta_col, gc_col=gc_col, egc=jnp.exp(gc_col), g_last=gc_col[c - 1:c, :],
                              attn=(kq[ic, j][c:] * decay).astype(BF16))
            pt[ic, hv] = eye_hi - (beta_col * kq[ic, j][:c] * decay) * strict_f
        for _ in range(n_square):
            for key in keys:
                cur = pt[key]
                pt[key] = jnp.dot(cur.astype(BF16), jnp.concatenate([cur, zeros_c(cw)], axis=0).astype(BF16),
                                  preferred_element_type=F32) + cur * hi_f
        sol = {}
        for ic, hv in keys:
            a = st[ic, hv]
            v = conv_silu(v_ref, vh_ref, vs_ref, wv_ref, ic, slice(hv * DN_HEAD, (hv + 1) * DN_HEAD), 2 * wq)
            rhs = jnp.concatenate([v * a["beta_col"], k[ic, hv // 2] * (a["beta_col"] * a["egc"])], axis=1)
            rhs = jnp.concatenate([zeros_c(2 * DN_HEAD), rhs], axis=0)
            sol[ic, hv] = jnp.dot(pt[ic, hv].astype(BF16), rhs.astype(BF16), preferred_element_type=F32)
        for ic, hv in keys:
            a = st[ic, hv]
            kh = k[ic, hv // 2]
            pre[ic, hv] = dict(
                u_base=sol[ic, hv][:, :DN_HEAD],
                kc_qd=jnp.concatenate([sol[ic, hv][:, DN_HEAD:], q[ic, hv // 2] * a["egc"]], axis=0).astype(BF16),
                attn=a["attn"],
                k_dec_t=(kh * jnp.exp(a["g_last"] - a["gc_col"])).T.astype(BF16),
                s_decay=jnp.exp(a["g_last"]))

    s = {hv: s_ref[0, hv] for hv in heads}
    for ic in range(n_chunks):
        ks_qs = {hv: jnp.dot(pre[ic, hv]["kc_qd"], s[hv].astype(BF16), preferred_element_type=F32) for hv in heads}
        u = {hv: pre[ic, hv]["u_base"] - ks_qs[hv][:c] for hv in heads}
        o = {hv: ks_qs[hv][c:] + jnp.dot(pre[ic, hv]["attn"],
                                         jnp.concatenate([u[hv], zeros_c(DN_HEAD)], axis=0).astype(BF16),
                                         preferred_element_type=F32) for hv in heads}
        for hv in heads:
            s[hv] = s[hv] * pre[ic, hv]["s_decay"] + jnp.dot(pre[ic, hv]["k_dec_t"], u[hv].astype(BF16),
                                                           preferred_element_type=F32)
        for hv in heads:
            lanes = slice(hv * DN_HEAD, (hv + 1) * DN_HEAD)
            on = o[hv] * lax.rsqrt(jnp.mean(o[hv] * o[hv], axis=-1, keepdims=True) + NORM_EPS) * nw
            z = z_ref[ic * c:ic * c + valid, lanes]
            o_ref[ic * c:ic * c + valid, lanes] = (on[:valid] * (z * jax.nn.sigmoid(z))).astype(o_ref.dtype)
    for hv in heads:
        s_ref[0, hv] = s[hv]


def _gated_delta(p_a, row0, bsz, t, conv_state, w_conv, g5, beta5, s0, norm_w, o_init, *, chunk, n_chunks, n_qk):
    rows = min(t, chunk * n_chunks)
    nblk = t // rows
    assert t % rows == 0 and row0 % rows == 0 and (rows == chunk * n_chunks or n_chunks == 1)
    assert rows % HALO == 0
    rb0 = row0 // rows
    wq, wv = DN_HEAD * n_qk, 2 * DN_HEAD * n_qk
    kq0, kv0, kz0 = DN_QK_DIM // wq, 2 * DN_QK_DIM // wv, OFF_Z // wv

    def rows_spec(width, col0):
        return pl.BlockSpec((rows, width), lambda ib, ih, it: (rb0 + ib * nblk + it, col0 + ih))

    def halo_spec(width, col0):
        return pl.BlockSpec((HALO, width), lambda ib, ih, it: (
            jnp.maximum((rb0 + ib * nblk + it) * (rows // HALO) - 1, 0), col0 + ih))

    def state_spec(width, col0):
        return pl.BlockSpec((1, HALO, width), lambda ib, ih, it: (ib, 0, col0 + ih))

    def w_spec(width, col0):
        return pl.BlockSpec((CONV_W, width), lambda ib, ih, it: (0, col0 + ih))

    g_spec = pl.BlockSpec((1, n_qk, 2) + g5.shape[3:], lambda ib, ih, it: (ib, ih, 0, 0, 0))
    s_spec = pl.BlockSpec((1, 2 * n_qk, DN_HEAD, DN_HEAD), lambda ib, ih, it: (ib, ih, 0, 0))
    in_specs = [rows_spec(wq, 0), rows_spec(wq, kq0), rows_spec(wv, kv0), rows_spec(wv, kz0),
                halo_spec(wq, 0), halo_spec(wq, kq0), halo_spec(wv, kv0),
                state_spec(wq, 0), state_spec(wq, kq0), state_spec(wv, kv0),
                w_spec(wq, 0), w_spec(wq, kq0), w_spec(wv, kv0),
                g_spec, g_spec, s_spec, pl.BlockSpec((1, DN_HEAD), lambda ib, ih, it: (0, 0))]
    args = [p_a, p_a, p_a, p_a, p_a, p_a, p_a, conv_state, conv_state, conv_state,
            w_conv, w_conv, w_conv, g5, beta5, s0, norm_w]
    aliases = {}
    if o_init is not None:
        in_specs.append(pl.BlockSpec(memory_space=pl.ANY))
        args.append(o_init)
        aliases = {len(args) - 1: 0}
    vmem = 2 * (2 * rows * wq * 4 + 2 * rows * wv * 4 + rows * wv * 2 + 4 * n_qk * DN_HEAD * DN_HEAD * 4) + (16 << 20)
    return pl.pallas_call(
        functools.partial(_gdn_kernel, chunk=chunk, n_chunks=n_chunks, n_qk=n_qk),
        grid=(bsz, DN_QK_HEADS // n_qk, nblk),
        in_specs=in_specs,
        out_specs=[rows_spec(wv, 0), s_spec],
        out_shape=[jax.ShapeDtypeStruct((p_a.shape[0], DN_V_DIM), BF16),
                   jax.ShapeDtypeStruct((bsz, DN_V_HEADS, DN_HEAD, DN_HEAD), F32)],
        scratch_shapes=[pltpu.VMEM((HALO + min(chunk, rows), 2 * wq + wv), F32)],
        input_output_aliases=aliases,
        compiler_params=_params(3, vmem),
        name="gated_delta_rule",
    )(*args)


def _lru_kernel(x_ref, y_ref, halo_ref, state_ref, wc_ref, bc_ref, wga_ref, bga_ref, wgx_ref, bgx_ref, sp_ref, h0_ref,
                *rest, rows):
    o_ref, hl_ref, a_s, b_s, h_s = rest[-5:]

    @pl.when(pl.program_id(1) == 0)
    def _():
        h_s[...] = h0_ref[0]

    for hd in range(LRU_HEADS):
        lanes = slice(hd * LRU_BW, (hd + 1) * LRU_BW)
        tail = jnp.where(pl.program_id(1) == 0, state_ref[0, :, lanes], halo_ref[:, lanes])
        xh = _causal_conv_rows(tail, x_ref[:, lanes], wc_ref[:, lanes]) + bc_ref[:, lanes]
        xh_bf = xh.astype(BF16)
        r = jax.nn.sigmoid(jnp.dot(xh_bf, wga_ref[hd].astype(BF16), preferred_element_type=F32) + bga_ref[hd])
        i = jax.nn.sigmoid(jnp.dot(xh_bf, wgx_ref[hd].astype(BF16), preferred_element_type=F32) + bgx_ref[hd])
        log_a = (-LRU_C) * r * sp_ref[:, lanes]
        a = jnp.exp(log_a)
        a_s[:, lanes] = a
        b_s[:, lanes] = jnp.sqrt(1.0 - a * a) * (i * xh)

    row = lax.broadcasted_iota(jnp.int32, (SUBLANES, LRU_W), 0)

    def group(ig, h):
        r0 = pl.multiple_of(ig * SUBLANES, SUBLANES)
        a = a_s[pl.ds(r0, SUBLANES), :]
        b = b_s[pl.ds(r0, SUBLANES), :]
        for sh in (1, 2, 4):
            keep = row >= sh
            a_sh = jnp.where(keep, pltpu.roll(a, sh, axis=0), 1.0)
            b_sh = jnp.where(keep, pltpu.roll(b, sh, axis=0), 0.0)
            b = a * b_sh + b
            a = a * a_sh
        hs = a * h + b
        gy = jax.nn.gelu(y_ref[pl.ds(r0, SUBLANES), :])
        o_ref[pl.ds(r0, SUBLANES), :] = (hs * gy).astype(o_ref.dtype)
        return hs[SUBLANES - 1:SUBLANES, :]

    h = lax.fori_loop(0, rows // SUBLANES, group, h_s[...])
    h_s[...] = h
    hl_ref[0] = h


def _rglru(p_r, row0, bsz, t, conv_state, w_conv, b_conv, w_ga, b_ga, w_gx, b_gx, sp, h0, o_init, *, rows):
    w = LRU_W
    nblk = t // rows
    assert t % rows == 0 and rows % SUBLANES == 0 and row0 % rows == 0
    rb0 = row0 // rows
    rows_spec = lambda col: pl.BlockSpec((rows, w), lambda ib, it: (rb0 + ib * nblk + it, col))
    full3 = lambda shape: pl.BlockSpec(shape, lambda ib, it: (0, 0, 0))
    full2 = lambda shape: pl.BlockSpec(shape, lambda ib, it: (0, 0))
    h_spec = pl.BlockSpec((1, 1, w), lambda ib, it: (ib, 0, 0))
    in_specs = [rows_spec(0), rows_spec(1),
                pl.BlockSpec((HALO, w), lambda ib, it: (
                    jnp.maximum((rb0 + ib * nblk + it) * (rows // HALO) - 1, 0), 0)),
                pl.BlockSpec((1, HALO, w), lambda ib, it: (ib, 0, 0)),
                full2((CONV_W, w)), full2((1, w)),
                full3((LRU_HEADS, LRU_BW, LRU_BW)), full3((LRU_HEADS, 1, LRU_BW)),
                full3((LRU_HEADS, LRU_BW, LRU_BW)), full3((LRU_HEADS, 1, LRU_BW)),
                full2((1, w)), h_spec]
    args = [p_r, p_r, p_r, conv_state, w_conv, b_conv, w_ga, b_ga, w_gx, b_gx, sp, h0]
    aliases = {}
    if o_init is not None:
        in_specs.append(pl.BlockSpec(memory_space=pl.ANY))
        args.append(o_init)
        aliases = {len(args) - 1: 0}
    vmem = 2 * (2 * rows * w * 4 + rows * w * 2 + 4 * LRU_HEADS * LRU_BW * LRU_BW * 4) + 2 * rows * w * 4 + (8 << 20)
    return pl.pallas_call(
        functools.partial(_lru_kernel, rows=rows),
        grid=(bsz, nblk),
        in_specs=in_specs,
        out_specs=[rows_spec(0), h_spec],
        out_shape=[jax.ShapeDtypeStruct((p_r.shape[0], w), BF16), jax.ShapeDtypeStruct((bsz, 1, w), F32)],
        scratch_shapes=[pltpu.VMEM((rows, w), F32), pltpu.VMEM((rows, w), F32), pltpu.VMEM((1, w), F32)],
        input_output_aliases=aliases,
        compiler_params=_params(2, vmem),
        name="rg_lru",
    )(*args)


def _merge_kernel(oa_ref, ob_ref, wa_ref, wb_ref, ga_ref, gb_ref, o_ref):
    ya = jnp.dot(oa_ref[...], wa_ref[...].astype(BF16), preferred_element_type=F32)
    yb = jnp.dot(ob_ref[...], wb_ref[...].astype(BF16), preferred_element_type=F32)
    o_ref[...] = (jax.nn.sigmoid(ga_ref[...]) * ya + jax.nn.sigmoid(gb_ref[...]) * yb).astype(o_ref.dtype)


def _merge(oa, ob, wa, wb, gates, ga_col0, gb_col0, *, tm, tn):
    m = oa.shape[0]
    n = wa.shape[1]
    assert m % tm == 0 and n % tn == 0 and ga_col0 % tn == 0 and gb_col0 % tn == 0
    ka, kb = oa.shape[1], ob.shape[1]
    vmem = 2 * (tm * (ka + kb) * 2 + (ka + kb) * tn * 4 + 2 * tm * tn * 4 + tm * tn * 2) + (ka + kb) * tn * 2
    row = lambda kdim: pl.BlockSpec((tm, kdim), lambda i, j: (i, 0))
    col = lambda kdim: pl.BlockSpec((kdim, tn), lambda i, j: (0, j))
    gate = lambda col0: pl.BlockSpec((tm, tn), lambda i, j: (i, col0 // tn + j))
    return pl.pallas_call(
        _merge_kernel,
        grid=(m // tm, n // tn),
        in_specs=[row(ka), row(kb), col(ka), col(kb), gate(ga_col0), gate(gb_col0)],
        out_specs=pl.BlockSpec((tm, tn), lambda i, j: (i, j)),
        out_shape=jax.ShapeDtypeStruct((m, n), BF16),
        compiler_params=_params(2, vmem),
        name="gated_merge",
    )(oa, ob, wa, wb, gates, gates)


def _pack_bf16_pairs(x):
    half = x.shape[1] // 2
    bits = lambda u: pltpu.bitcast(u.astype(BF16).astype(F32), jnp.uint32)
    return (bits(x[:, :half]) >> 16) | (bits(x[:, half:]) & jnp.uint32(0xFFFF0000))


def _unpack_bf16_pairs(words):
    lo = pltpu.bitcast(words << 16, F32)
    hi = pltpu.bitcast(words & jnp.uint32(0xFFFF0000), F32)
    return jnp.concatenate([lo, hi], axis=1).astype(BF16)


def _out_kernel(m_ref, x_ref, w_ref, g_ref, b_ref, wr_ref, br_ref, h_ref, hp_ref, lg_ref):
    y = jnp.dot(m_ref[...], w_ref[...].astype(BF16), preferred_element_type=F32)
    h = _layer_norm(DEEPNORM_ALPHA * x_ref[...] + y, g_ref[...], b_ref[...])
    h_ref[...] = h
    hp_ref[...] = _pack_bf16_pairs(h)
    lg_ref[...] = jnp.dot(h, wr_ref[...], preferred_element_type=F32,
                          precision=lax.Precision.HIGHEST) + br_ref[...]


def _out_proj(merged, x, w_out, g, b, w_router, b_router, *, tm):
    m, d = x.shape
    e = w_router.shape[1]
    assert m % tm == 0
    vmem = 2 * (tm * d * 2 + tm * d * 4 + d * d * 4 + tm * d * 6 + d * e * 4 + tm * 128 * 4) + d * d * 2
    rows = lambda width: pl.BlockSpec((tm, width), lambda i: (i, 0))
    full = lambda shape: pl.BlockSpec(shape, lambda i: (0, 0))
    return pl.pallas_call(
        _out_kernel,
        grid=(m // tm,),
        in_specs=[rows(d), rows(d), full((d, d)), full((1, d)), full((1, d)), full((d, e)), full((1, e))],
        out_specs=[rows(d), rows(d // 2), rows(e)],
        out_shape=[jax.ShapeDtypeStruct((m, d), F32), jax.ShapeDtypeStruct((m, d // 2), jnp.uint32),
                   jax.ShapeDtypeStruct((m, e), F32)],
        compiler_params=_params(1, vmem),
        name="out_proj_ln_router",
    )(merged, x, w_out, g, b, w_router, b_router)


def _moe_kernel(ge_ref, gn_ref, gofs_ref, tok_ref, h_hbm, wg_ref, wu_ref, bg_ref, bu_ref, wd_ref, bd_ref, o_ref,
                x_buf, sem, wg_s, wu_s, wd_s):
    ig = pl.program_id(0)
    jf = pl.program_id(1)
    n_valid = gn_ref[ig]
    slot = ig % 2

    def row_copy(g, buf_slot, r):
        tok = tok_ref[gofs_ref[g] + jnp.minimum(r, jnp.maximum(gn_ref[g] - 1, 0))]
        return pltpu.make_async_copy(h_hbm.at[pl.ds(tok, 1), :], x_buf.at[buf_slot, pl.ds(r, 1), :],
                                     sem.at[buf_slot])

    def for_rows(g, buf_slot, fn):
        n_fetch = pl.cdiv(gn_ref[g], MOE_SUB_ROWS) * MOE_SUB_ROWS

        def body(r, carry):
            fn(row_copy(g, buf_slot, r))
            return carry
        lax.fori_loop(0, n_fetch, body, 0)

    @pl.when(jf == 0)
    def _():
        @pl.when(ig == 0)
        def _():
            for_rows(0, 0, lambda cp: cp.start())

        @pl.when(ig + 1 < pl.num_programs(0))
        def _():
            for_rows(ig + 1, (ig + 1) % 2, lambda cp: cp.start())

        for_rows(ig, slot, lambda cp: cp.wait())

    @pl.when((n_valid == 0) & (jf == 0))
    def _():
        o_ref[...] = jnp.zeros(o_ref.shape, o_ref.dtype)

    @pl.when(n_valid > 0)
    def _():
        @pl.when(jf == 0)
        def _():
            o_ref[...] = jnp.broadcast_to(bd_ref[0], o_ref.shape)

        wg_s[...] = wg_ref[0].astype(BF16)
        wu_s[...] = wu_ref[0].astype(BF16)
        wd_s[...] = wd_ref[0].astype(BF16)
        for sb in range(MOE_GROUP_ROWS // MOE_SUB_ROWS):
            @pl.when(sb * MOE_SUB_ROWS < n_valid)
            def _():
                rows = slice(sb * MOE_SUB_ROWS, (sb + 1) * MOE_SUB_ROWS)
                xb = _unpack_bf16_pairs(x_buf[slot, rows, :])
                hg = jnp.dot(xb, wg_s[...], preferred_element_type=F32) + bg_ref[0]
                hu = jnp.dot(xb, wu_s[...], preferred_element_type=F32) + bu_ref[0]
                hg = jnp.minimum(hg, SWIGLU_LIMIT)
                hu = jnp.clip(hu, -SWIGLU_LIMIT, SWIGLU_LIMIT)
                act = (hu + 1.0) * hg * jax.nn.sigmoid(SWIGLU_ALPHA * hg)
                o_ref[rows, :] += jnp.dot(act.astype(BF16), wd_s[...], preferred_element_type=F32)


def _moe_experts(h_packed, group_expert, group_valid, group_offset, sorted_tok, w_gu, b_gu, w_dn, b_dn):
    n_groups = group_expert.shape[0]
    d = 2 * h_packed.shape[1]
    nf = D_FF // MOE_FF_TILE
    gm, tf = MOE_GROUP_ROWS, MOE_FF_TILE

    def f_eff(ig, jf, gn):
        return jnp.where(gn[ig] > 0, jf, nf - 1)

    grid_spec = pltpu.PrefetchScalarGridSpec(
        num_scalar_prefetch=4,
        grid=(n_groups, nf),
        in_specs=[
            pl.BlockSpec(memory_space=pl.ANY),
            pl.BlockSpec((1, d, tf), lambda ig, jf, ge, gn, go, tk: (ge[ig], 0, f_eff(ig, jf, gn))),
            pl.BlockSpec((1, d, tf), lambda ig, jf, ge, gn, go, tk: (ge[ig], 0, nf + f_eff(ig, jf, gn))),
            pl.BlockSpec((1, 1, tf), lambda ig, jf, ge, gn, go, tk: (ge[ig], 0, f_eff(ig, jf, gn))),
            pl.BlockSpec((1, 1, tf), lambda ig, jf, ge, gn, go, tk: (ge[ig], 0, nf + f_eff(ig, jf, gn))),
            pl.BlockSpec((1, tf, d), lambda ig, jf, ge, gn, go, tk: (ge[ig], f_eff(ig, jf, gn), 0)),
            pl.BlockSpec((1, 1, d), lambda ig, jf, ge, gn, go, tk: (ge[ig], 0, 0)),
        ],
        out_specs=pl.BlockSpec((gm, d), lambda ig, jf, ge, gn, go, tk: (ig, 0)),
        scratch_shapes=[pltpu.VMEM((2, gm, d // 2), jnp.uint32), pltpu.SemaphoreType.DMA((2,)),
                        pltpu.VMEM((d, tf), BF16), pltpu.VMEM((d, tf), BF16), pltpu.VMEM((tf, d), BF16)],
    )
    vmem = 2 * gm * (d // 2) * 4 + 2 * (3 * d * tf * 4 + gm * d * 4) + 3 * d * tf * 2 + (6 << 20)
    return pl.pallas_call(
        _moe_kernel,
        grid_spec=grid_spec,
        out_shape=jax.ShapeDtypeStruct((n_groups * gm, d), F32),
        compiler_params=_params(2, vmem),
        name="moe_experts",
    )(group_expert, group_valid, group_offset, sorted_tok, h_packed, w_gu, w_gu, b_gu, b_gu, w_dn, b_dn)


def _combine_kernel(dest_ref, rows_hbm, gate_ref, h_ref, g_ref, b_ref, o_ref, buf, sem, *, tm):
    i = pl.program_id(0)
    n_steps = pl.num_programs(0)

    def row_copy(step, slot, t, kk):
        src = dest_ref[(step * tm + t) * TOP_K + kk]
        return pltpu.make_async_copy(rows_hbm.at[pl.ds(src, 1), :], buf.at[slot, pl.ds(kk * tm + t, 1), :],
                                     sem.at[slot])

    def for_all_rows(step, slot, fn):
        def body(t, carry):
            for kk in range(TOP_K):
                fn(row_copy(step, slot, t, kk))
            return carry
        lax.fori_loop(0, tm, body, 0)

    @pl.when(i == 0)
    def _():
        for_all_rows(0, 0, lambda cp: cp.start())

    @pl.when(i + 1 < n_steps)
    def _():
        for_all_rows(i + 1, (i + 1) % 2, lambda cp: cp.start())

    slot = i % 2
    for_all_rows(i, slot, lambda cp: cp.wait())
    y = sum(gate_ref[:, kk:kk + 1] * buf[slot, kk * tm:(kk + 1) * tm, :] for kk in range(TOP_K))
    o_ref[...] = _layer_norm(DEEPNORM_ALPHA * h_ref[...] + y, g_ref[...], b_ref[...])


def _combine(out_rows, dest, gate, h, g, b, *, tm):
    n, d = h.shape
    assert n % tm == 0
    grid_spec = pltpu.PrefetchScalarGridSpec(
        num_scalar_prefetch=1,
        grid=(n // tm,),
        in_specs=[pl.BlockSpec(memory_space=pl.ANY),
                  pl.BlockSpec((tm, TOP_K), lambda i, dest: (i, 0)),
                  pl.BlockSpec((tm, d), lambda i, dest: (i, 0)),
                  pl.BlockSpec((1, d), lambda i, dest: (0, 0)),
                  pl.BlockSpec((1, d), lambda i, dest: (0, 0))],
        out_specs=pl.BlockSpec((tm, d), lambda i, dest: (i, 0)),
        scratch_shapes=[pltpu.VMEM((2, TOP_K * tm, d), F32), pltpu.SemaphoreType.DMA((2,))],
    )
    vmem = 2 * TOP_K * tm * d * 4 + 2 * (2 * tm * d * 4 + tm * 128 * 4) + (4 << 20)
    return pl.pallas_call(
        functools.partial(_combine_kernel, tm=tm),
        grid_spec=grid_spec,
        out_shape=jax.ShapeDtypeStruct((n, d), F32),
        compiler_params=_params(1, vmem),
        name="moe_combine_ln",
    )(dest, out_rows, gate, h, g, b)


def _moe_ffn_ln(h, h_packed, logits, w_gu, b_gu, w_dn, b_dn, ln_g, ln_b):
    n, d = h.shape
    nk = n * TOP_K
    gm = MOE_GROUP_ROWS
    n_groups = N_EXPERTS + nk // gm
    top_logit, top_e = lax.top_k(logits, TOP_K)
    gate = jax.nn.softmax(top_logit, axis=-1)
    flat_e = top_e.reshape(nk)
    onehot = (flat_e[:, None] == jnp.arange(N_EXPERTS, dtype=flat_e.dtype)[None, :]).astype(jnp.int32)
    rank = jnp.take_along_axis(jnp.cumsum(onehot, axis=0), flat_e[:, None], axis=1)[:, 0] - 1
    counts = jnp.sum(onehot, axis=0)
    start_e = jnp.cumsum(counts) - counts
    groups_e = (counts + gm - 1) // gm
    gend_e = jnp.cumsum(groups_e)
    gstart_e = gend_e - groups_e
    dest = (gstart_e[flat_e] * gm + rank).astype(jnp.int32)
    sorted_tok = (jnp.argsort(flat_e, stable=True) // TOP_K).astype(jnp.int32)
    n_used = gend_e[-1]
    gidx = jnp.arange(n_groups, dtype=jnp.int32)
    gclamp = jnp.minimum(gidx, n_used - 1)
    group_expert = jnp.minimum(jnp.searchsorted(gend_e, gclamp, side="right"), N_EXPERTS - 1).astype(jnp.int32)
    within = gclamp - gstart_e[group_expert]
    used = gidx < n_used
    group_valid = jnp.where(used, jnp.clip(counts[group_expert] - within * gm, 0, gm), 0).astype(jnp.int32)
    group_offset = jnp.where(used, start_e[group_expert] + within * gm, 0).astype(jnp.int32)
    out_rows = _moe_experts(h_packed, group_expert, group_valid, group_offset, sorted_tok,
                            w_gu, b_gu.reshape(N_EXPERTS, 1, -1), w_dn, b_dn.reshape(N_EXPERTS, 1, -1))
    return _combine(out_rows, dest, gate, h, ln_g, ln_b, tm=COMBINE_ROWS)


def _pad_state(state):
    return jnp.pad(state, ((0, 0), (HALO - (CONV_W - 1), 0), (0, 0)))


def _last_rows(p, width, row0, bsz, t):
    if bsz == 1:
        return p[row0 + t - (CONV_W - 1):row0 + t, :width][None]
    return p[row0:row0 + bsz * t, :width].reshape(bsz, t, width)[:, t - (CONV_W - 1):]


def _dn_gates(p_ba, row0, bsz, t, chunk, a_log, dt_bias):
    ba = p_ba[row0:row0 + bsz * t, :2 * DN_V_HEADS].reshape(bsz, t, 2 * DN_V_HEADS)
    beta = jax.nn.sigmoid(ba[..., :DN_V_HEADS])
    g = -jnp.exp(a_log) * jax.nn.softplus(ba[..., DN_V_HEADS:] + dt_bias)
    n_ch = -(-t // chunk)

    def to5(u):
        u = jnp.pad(u, ((0, 0), (0, n_ch * chunk - t), (0, 0)))
        u = jnp.transpose(u, (0, 2, 1)).reshape(bsz, DN_QK_HEADS, 2, n_ch, chunk)
        return jnp.pad(u, ((0, 0),) * 4 + ((0, chunk),))

    return to5(g), to5(beta)


def kernel(x_prompt, x_sample, state_dn_conv, state_dn_ssm, state_lru_conv, state_lru_h, w_in, w_dn_conv, dn_a_log, dn_dt_bias, dn_norm_w, w_lru_conv, b_lru_conv, w_lru_ga, b_lru_ga, w_lru_gx, b_lru_gx, lru_lambda, w_proj_a, w_proj_b, w_out, ln1_g, ln1_b, w_router, b_router, w_gate_up, b_gate_up, w_down, b_down, ln2_g, ln2_b):
    assert w_in.shape[0] == 1, "single layer"
    bp, tp, d = x_prompt.shape
    bs, ts, _ = x_sample.shape
    np_, ns = bp * tp, bs * ts
    n = np_ + ns

    x_all = jnp.concatenate([x_prompt.reshape(np_, d), x_sample.reshape(ns, d)], axis=0)
    x_bf = x_all.astype(BF16)
    w_in0 = w_in[0]
    tm = n // 5
    p_a = _matmul(x_bf, w_in0, tm=tm, tn=1024, n_cols=OFF_B)
    p_ba = _matmul(x_bf, w_in0, tm=tm, tn=LANES, col0=OFF_B, n_cols=LANES)
    p_r = _matmul(x_bf, w_in0, tm=tm, tn=512, col0=OFF_LX)

    zeros = lambda *shape: jnp.zeros(shape, F32)
    norm_w = dn_norm_w[0].reshape(1, DN_HEAD)
    g5, beta5 = _dn_gates(p_ba, 0, bp, tp, GDN_CHUNK, dn_a_log[0], dn_dt_bias[0])
    o_a, dn_s_p = _gated_delta(
        p_a, 0, bp, tp, zeros(bp, HALO, DN_QKV_DIM), w_dn_conv[0], g5, beta5,
        zeros(bp, DN_V_HEADS, DN_HEAD, DN_HEAD), norm_w, jnp.zeros((n, DN_V_DIM), BF16),
        chunk=GDN_CHUNK, n_chunks=GDN_CHUNKS_PER_STEP, n_qk=GDN_QK_HEADS_PER_STEP)
    g5, beta5 = _dn_gates(p_ba, np_, bs, ts, GDN_CHUNK, dn_a_log[0], dn_dt_bias[0])
    o_a, dn_s_s = _gated_delta(
        p_a, np_, bs, ts, _pad_state(state_dn_conv[0]), w_dn_conv[0], g5, beta5, state_dn_ssm[0], norm_w, o_a,
        chunk=GDN_CHUNK, n_chunks=1, n_qk=4)

    lru_args = (w_lru_conv[0], b_lru_conv, w_lru_ga[0], b_lru_ga[0].reshape(LRU_HEADS, 1, LRU_BW),
                w_lru_gx[0], b_lru_gx[0].reshape(LRU_HEADS, 1, LRU_BW),
                jax.nn.softplus(-lru_lambda[0]).reshape(1, LRU_W))
    o_b, h_p = _rglru(p_r, 0, bp, tp, zeros(bp, HALO, LRU_W), *lru_args, zeros(bp, 1, LRU_W),
                      jnp.zeros((n, LRU_W), BF16), rows=256)
    o_b, h_s = _rglru(p_r, np_, bs, ts, _pad_state(state_lru_conv[0]), *lru_args,
                      state_lru_h[0].reshape(bs, 1, LRU_W), o_b, rows=ts)

    merged = _merge(o_a, o_b, w_proj_a[0], w_proj_b[0], p_r, 2 * LRU_W, 3 * LRU_W, tm=n // 10, tn=256)
    h, h_packed, logits = _out_proj(merged, x_all, w_out[0], ln1_g, ln1_b, w_router[0], b_router, tm=n // 26)
    out = _moe_ffn_ln(h, h_packed, logits, w_gate_up[0], b_gate_up[0], w_down[0], b_down[0], ln2_g, ln2_b)

    return (out[:np_].reshape(bp, tp, d), out[np_:].reshape(bs, ts, d),
            _last_rows(p_a, DN_QKV_DIM, 0, bp, tp)[None], dn_s_p[None],
            _last_rows(p_r, LRU_W, 0, bp, tp)[None], h_p.reshape(bp, LRU_W)[None],
            _last_rows(p_a, DN_QKV_DIM, np_, bs, ts)[None], dn_s_s[None],
            _last_rows(p_r, LRU_W, np_, bs, ts)[None], h_s.reshape(bs, LRU_W)[None])
```

```python
import functools

import jax
import jax.numpy as jnp
from jax import lax
from jax.experimental import pallas as pl
from jax.experimental.pallas import tpu as pltpu

F32 = jnp.float32
BF16 = jnp.bfloat16

D_MODEL = 2048
DN_QK_HEADS = 16
DN_V_HEADS = 32
DN_HEAD = 128
DN_QK_DIM = DN_QK_HEADS * DN_HEAD
DN_V_DIM = DN_V_HEADS * DN_HEAD
DN_QKV_DIM = 2 * DN_QK_DIM + DN_V_DIM
CONV_W = 4
LRU_W = D_MODEL
LRU_HEADS = 16
LRU_BW = LRU_W // LRU_HEADS
LRU_C = 8.0
N_EXPERTS = 32
TOP_K = 4
D_FF = D_MODEL
SWIGLU_LIMIT = 7.0
SWIGLU_ALPHA = 1.702
DEEPNORM_ALPHA = 2.0 ** 0.25
LN_EPS = 1e-5
NORM_EPS = 1e-6
OFF_Z = DN_QKV_DIM
OFF_B = OFF_Z + DN_V_DIM
OFF_A = OFF_B + DN_V_HEADS
OFF_LX = OFF_A + DN_V_HEADS
OFF_LY = OFF_LX + LRU_W
OFF_GA = OFF_LY + LRU_W
OFF_GB = OFF_GA + D_MODEL

V7X_VMEM_LIMIT_BYTES = 56 * 1024 * 1024
SUBLANES = 8
LANES = 128

HALO = SUBLANES
GDN_CHUNK = 64
GDN_CHUNKS_PER_STEP = 4
GDN_QK_HEADS_PER_STEP = 4
GDN_WAVE_CHAINS = 16
MOE_GROUP_ROWS = 1280
MOE_SUB_BLOCKS = ((0, 256), (256, 256), (512, 256), (768, 256), (1024, 128), (1152, 128))
MOE_FF_TILE = 256
COMBINE_ROWS = 128
DMA_ISSUE_UNROLL = 8


def _params(n_axes, vmem_bytes):
    limit = min(V7X_VMEM_LIMIT_BYTES, int(vmem_bytes * 1.2) + (4 << 20))
    return pltpu.CompilerParams(dimension_semantics=("arbitrary",) * n_axes, vmem_limit_bytes=limit)


def _layer_norm(x, g, b):
    mu = jnp.mean(x, axis=-1, keepdims=True)
    xc = x - mu
    var = jnp.mean(xc * xc, axis=-1, keepdims=True)
    return xc * lax.rsqrt(var + LN_EPS) * g + b


def _causal_conv_rows(tail, x, w):
    r = x.shape[0]
    xx = jnp.concatenate([tail, x], axis=0)
    first = HALO - (CONV_W - 1)
    return sum(xx[first + j:first + j + r] * w[j:j + 1] for j in range(CONV_W))


def _mm_kernel(x_ref, w_ref, o_ref):
    o_ref[...] = jnp.dot(x_ref[...], w_ref[...].astype(BF16),
                         preferred_element_type=F32).astype(o_ref.dtype)


def _mm_shift_kernel(x_ref, wa_ref, wb_ref, o_ref, *, shift):
    w = jnp.concatenate([wa_ref[:, shift:], wb_ref[:, :shift]], axis=1)
    o_ref[...] = jnp.dot(x_ref[...], w.astype(BF16), preferred_element_type=F32).astype(o_ref.dtype)


def _matmul(x_bf, w, *, tm, tn, col0=0, n_cols=None):
    m, k = x_bf.shape
    n = w.shape[1] - col0 if n_cols is None else n_cols
    shift = col0 % tn
    assert m % tm == 0 and n % tn == 0 and shift % 64 == 0 and shift <= LANES
    vmem = 2 * (tm * k * 2 + k * (tn + LANES) * 4 + tm * tn * 4) + k * tn * 6
    x_spec = pl.BlockSpec((tm, k), lambda i, j: (i, 0))
    w_spec = pl.BlockSpec((k, tn), lambda i, j: (0, col0 // tn + j))
    if shift == 0:
        body, w_specs, w_args = _mm_kernel, [w_spec], [w]
    else:
        body = functools.partial(_mm_shift_kernel, shift=shift)
        w_specs = [w_spec, pl.BlockSpec((k, LANES), lambda i, j: (0, (col0 // tn + j + 1) * (tn // LANES)))]
        w_args = [w, w]
    return pl.pallas_call(
        body,
        grid=(m // tm, n // tn),
        in_specs=[x_spec] + w_specs,
        out_specs=pl.BlockSpec((tm, tn), lambda i, j: (i, j)),
        out_shape=jax.ShapeDtypeStruct((m, n), F32),
        compiler_params=_params(2, vmem),
        name="dense_matmul",
    )(x_bf, *w_args)


def _dot_nt(a, b):
    return lax.dot_general(a, b, (((1,), (1,)), ((), ())), preferred_element_type=F32)


def _dot_tn(a, b):
    return lax.dot_general(a, b, (((0,), (0,)), ((), ())), preferred_element_type=F32)


def _l2norm(u):
    return u * lax.rsqrt(jnp.sum(u * u, axis=-1, keepdims=True) + NORM_EPS)


def _gdn_kernel(q_ref, k_ref, v_ref, z_ref, qh_ref, kh_ref, vh_ref, qs_ref, ks_ref, vs_ref, wq_ref, wk_ref, wv_ref,
                g_ref, beta_ref, s0_ref, nw_ref, *rest, chunk, n_chunks, n_qk):
    o_ref, s_ref, hist_ref = rest[-3:]
    c = chunk

    @pl.when(pl.program_id(2) == 0)
    def _():
        s_ref[...] = s0_ref[...]

    cw = 2 * c
    ri = lax.broadcasted_iota(jnp.int32, (c, cw), 0)
    ci = lax.broadcasted_iota(jnp.int32, (c, cw), 1)
    as_f = lambda m: jnp.where(m, 1.0, 0.0).astype(F32)
    causal_f, strict_f, eye_f = as_f(ri >= ci), as_f(ri > ci), as_f(ri == ci)
    le_f, hi_f, eye_hi = as_f(ri <= ci), as_f(ci >= c), as_f(ci == ri + c)
    n_square = c.bit_length() - 1
    nw = nw_ref[...]
    valid = min(c, q_ref.shape[0])
    zeros_c = lambda width: jnp.zeros((c, width), F32)

    def col_of(row):
        return jnp.sum(row * eye_f, axis=1, keepdims=True)

    def conv_silu(x_ref, halo_ref, state_ref, w_ref, ic, lanes, hist_col0):
        lo = ic * c
        if ic == 0:
            cols = slice(hist_col0 + lanes.start, hist_col0 + lanes.stop)
            hist_ref[0:HALO, cols] = jnp.where(pl.program_id(2) == 0, state_ref[0, :, lanes], halo_ref[:, lanes])
            hist_ref[HALO:HALO + valid, cols] = x_ref[0:valid, lanes]
            src, first, src_lanes = hist_ref, HALO - (CONV_W - 1), cols
        else:
            src, first, src_lanes = x_ref, lo - (CONV_W - 1), lanes
        y = sum(src[first + j:first + j + valid, src_lanes] * w_ref[j:j + 1, lanes] for j in range(CONV_W))
        y = y / (1.0 + jnp.exp(-y))
        return y if valid == c else jnp.concatenate([y, jnp.zeros((c - valid, y.shape[1]), F32)], axis=0)

    heads = range(2 * n_qk)
    wq = n_qk * DN_HEAD
    pre = {}
    wave_chunks = max(1, GDN_WAVE_CHAINS // (2 * n_qk))
    for w0 in range(0, n_chunks, wave_chunks):
        ics = range(w0, min(w0 + wave_chunks, n_chunks))
        qkeys = [(ic, j) for ic in ics for j in range(n_qk)]
        keys = [(ic, hv) for ic in ics for hv in heads]
        q, k = {}, {}
        for ic, j in qkeys:
            qk_lanes = slice(j * DN_HEAD, (j + 1) * DN_HEAD)
            q[ic, j] = _l2norm(conv_silu(q_ref, qh_ref, qs_ref, wq_ref, ic, qk_lanes, 0)) * (DN_HEAD ** -0.5)
            k[ic, j] = _l2norm(conv_silu(k_ref, kh_ref, ks_ref, wk_ref, ic, qk_lanes, wq))
        kq = {}
        for key in qkeys:
            k_t = jnp.concatenate([k[key], zeros_c(DN_HEAD)], axis=0).T.astype(BF16)
            kq[key] = jnp.dot(jnp.concatenate([k[key], q[key]], axis=0).astype(BF16), k_t,
                              preferred_element_type=F32)
        st, pt = {}, {}
        for ic, hv in keys:
            j, hh = divmod(hv, 2)
            chunk_row = pl.ds(pl.program_id(2) * n_chunks + ic, 1)
            g_row = g_ref[0, j, hh, chunk_row, :]
            beta_col = col_of(beta_ref[0, j, hh, chunk_row, :])
            gc_col = jnp.sum(g_row * causal_f, axis=1, keepdims=True)
            gc_row = jnp.sum(col_of(g_row) * le_f, axis=0, keepdims=True)
            decay = jnp.exp(jnp.minimum(gc_col - gc_row, 0.0)) * causal_f
            st[ic, hv] = dict(beta_col=beta_col, gc_col=gc_col, egc=jnp.exp(gc_col), g_last=gc_col[c - 1:c, :],
                              attn=(kq[ic, j][c:] * decay).astype(BF16))
            pt[ic, hv] = eye_hi - (beta_col * kq[ic, j][:c] * decay) * strict_f
        for _ in range(n_square):
            for key in keys:
                cur = pt[key]
                pt[key] = jnp.dot(cur.astype(BF16), jnp.concatenate([cur, zeros_c(cw)], axis=0).astype(BF16),
                                  preferred_element_type=F32) + cur * hi_f
        sol = {}
        for ic, hv in keys:
            a = st[ic, hv]
            v = conv_silu(v_ref, vh_ref, vs_ref, wv_ref, ic, slice(hv * DN_HEAD, (hv + 1) * DN_HEAD), 2 * wq)
            rhs = jnp.concatenate([v * a["beta_col"], k[ic, hv // 2] * (a["beta_col"] * a["egc"])], axis=1)
            rhs = jnp.concatenate([zeros_c(2 * DN_HEAD), rhs], axis=0)
            sol[ic, hv] = jnp.dot(pt[ic, hv].astype(BF16), rhs.astype(BF16), preferred_element_type=F32)
        for ic, hv in keys:
            a = st[ic, hv]
            kh = k[ic, hv // 2]
            pre[ic, hv] = dict(
                u_base=sol[ic, hv][:, :DN_HEAD],
                kc_qd=jnp.concatenate([sol[ic, hv][:, DN_HEAD:], q[ic, hv // 2] * a["egc"]], axis=0).astype(BF16),
                attn=a["attn"],
                k_dec_t=(kh * jnp.exp(a["g_last"] - a["gc_col"])).T.astype(BF16),
                s_decay=jnp.exp(a["g_last"]))

    s = {hv: s_ref[0, hv] for hv in heads}
    for ic in range(n_chunks):
        ks_qs = {hv: jnp.dot(pre[ic, hv]["kc_qd"], s[hv].astype(BF16), preferred_element_type=F32) for hv in heads}
        u = {hv: pre[ic, hv]["u_base"] - ks_qs[hv][:c] for hv in heads}
        o = {hv: ks_qs[hv][c:] + jnp.dot(pre[ic, hv]["attn"],
                                         jnp.concatenate([u[hv], zeros_c(DN_HEAD)], axis=0).astype(BF16),
                                         preferred_element_type=F32) for hv in heads}
        for hv in heads:
            s[hv] = s[hv] * pre[ic, hv]["s_decay"] + jnp.dot(pre[ic, hv]["k_dec_t"], u[hv].astype(BF16),
                                                           preferred_element_type=F32)
        for hv in heads:
            lanes = slice(hv * DN_HEAD, (hv + 1) * DN_HEAD)
            on = o[hv] * lax.rsqrt(jnp.mean(o[hv] * o[hv], axis=-1, keepdims=True) + NORM_EPS) * nw
            z = z_ref[ic * c:ic * c + valid, lanes]
            o_ref[ic * c:ic * c + valid, lanes] = (on[:valid] * (z * jax.nn.sigmoid(z))).astype(o_ref.dtype)
    for hv in heads:
        s_ref[0, hv] = s[hv]


def _gated_delta(p_a, row0, bsz, t, conv_state, w_conv, g5, beta5, s0, norm_w, o_init, *, chunk, n_chunks, n_qk):
    rows = min(t, chunk * n_chunks)
    nblk = t // rows
    assert t % rows == 0 and row0 % rows == 0 and (rows == chunk * n_chunks or n_chunks == 1)
    assert rows % HALO == 0
    rb0 = row0 // rows
    wq, wv = DN_HEAD * n_qk, 2 * DN_HEAD * n_qk
    kq0, kv0, kz0 = DN_QK_DIM // wq, 2 * DN_QK_DIM // wv, OFF_Z // wv

    def rows_spec(width, col0):
        return pl.BlockSpec((rows, width), lambda ib, ih, it: (rb0 + ib * nblk + it, col0 + ih))

    def halo_spec(width, col0):
        return pl.BlockSpec((HALO, width), lambda ib, ih, it: (
            jnp.maximum((rb0 + ib * nblk + it) * (rows // HALO) - 1, 0), col0 + ih))

    def state_spec(width, col0):
        return pl.BlockSpec((1, HALO, width), lambda ib, ih, it: (ib, 0, col0 + ih))

    def w_spec(width, col0):
        return pl.BlockSpec((CONV_W, width), lambda ib, ih, it: (0, col0 + ih))

    g_spec = pl.BlockSpec((1, n_qk, 2) + g5.shape[3:], lambda ib, ih, it: (ib, ih, 0, 0, 0))
    s_spec = pl.BlockSpec((1, 2 * n_qk, DN_HEAD, DN_HEAD), lambda ib, ih, it: (ib, ih, 0, 0))
    in_specs = [rows_spec(wq, 0), rows_spec(wq, kq0), rows_spec(wv, kv0), rows_spec(wv, kz0),
                halo_spec(wq, 0), halo_spec(wq, kq0), halo_spec(wv, kv0),
                state_spec(wq, 0), state_spec(wq, kq0), state_spec(wv, kv0),
                w_spec(wq, 0), w_spec(wq, kq0), w_spec(wv, kv0),
                g_spec, g_spec, s_spec, pl.BlockSpec((1, DN_HEAD), lambda ib, ih, it: (0, 0))]
    args = [p_a, p_a, p_a, p_a, p_a, p_a, p_a, conv_state, conv_state, conv_state,
            w_conv, w_conv, w_conv, g5, beta5, s0, norm_w]
    aliases = {}
    if o_init is not None:
        in_specs.append(pl.BlockSpec(memory_space=pl.ANY))
        args.append(o_init)
        aliases = {len(args) - 1: 0}
    vmem = 2 * (2 * rows * wq * 4 + 2 * rows * wv * 4 + rows * wv * 2 + 4 * n_qk * DN_HEAD * DN_HEAD * 4) + (16 << 20)
    return pl.pallas_call(
        functools.partial(_gdn_kernel, chunk=chunk, n_chunks=n_chunks, n_qk=n_qk),
        grid=(bsz, DN_QK_HEADS // n_qk, nblk),
        in_specs=in_specs,
        out_specs=[rows_spec(wv, 0), s_spec],
        out_shape=[jax.ShapeDtypeStruct((p_a.shape[0], DN_V_DIM), BF16),
                   jax.ShapeDtypeStruct((bsz, DN_V_HEADS, DN_HEAD, DN_HEAD), F32)],
        scratch_shapes=[pltpu.VMEM((HALO + min(chunk, rows), 2 * wq + wv), F32)],
        input_output_aliases=aliases,
        compiler_params=_params(3, vmem),
        name="gated_delta_rule",
    )(*args)


def _lru_kernel(x_ref, y_ref, halo_ref, state_ref, wc_ref, bc_ref, wga_ref, bga_ref, wgx_ref, bgx_ref, sp_ref, h0_ref,
                *rest, rows):
    o_ref, hl_ref, a_s, b_s, h_s = rest[-5:]

    @pl.when(pl.program_id(1) == 0)
    def _():
        h_s[...] = h0_ref[0]

    for hd in range(LRU_HEADS):
        lanes = slice(hd * LRU_BW, (hd + 1) * LRU_BW)
        tail = jnp.where(pl.program_id(1) == 0, state_ref[0, :, lanes], halo_ref[:, lanes])
        xh = _causal_conv_rows(tail, x_ref[:, lanes], wc_ref[:, lanes]) + bc_ref[:, lanes]
        xh_bf = xh.astype(BF16)
        r = jax.nn.sigmoid(jnp.dot(xh_bf, wga_ref[hd].astype(BF16), preferred_element_type=F32) + bga_ref[hd])
        i = jax.nn.sigmoid(jnp.dot(xh_bf, wgx_ref[hd].astype(BF16), preferred_element_type=F32) + bgx_ref[hd])
        log_a = (-LRU_C) * r * sp_ref[:, lanes]
        a = jnp.exp(log_a)
        a_s[:, lanes] = a
        b_s[:, lanes] = jnp.sqrt(1.0 - a * a) * (i * xh)

    row = lax.broadcasted_iota(jnp.int32, (SUBLANES, LRU_W), 0)

    def group(ig, h):
        r0 = pl.multiple_of(ig * SUBLANES, SUBLANES)
        a = a_s[pl.ds(r0, SUBLANES), :]
        b = b_s[pl.ds(r0, SUBLANES), :]
        for sh in (1, 2, 4):
            keep = row >= sh
            a_sh = jnp.where(keep, pltpu.roll(a, sh, axis=0), 1.0)
            b_sh = jnp.where(keep, pltpu.roll(b, sh, axis=0), 0.0)
            b = a * b_sh + b
            a = a * a_sh
        hs = a * h + b
        gy = jax.nn.gelu(y_ref[pl.ds(r0, SUBLANES), :])
        o_ref[pl.ds(r0, SUBLANES), :] = (hs * gy).astype(o_ref.dtype)
        return hs[SUBLANES - 1:SUBLANES, :]

    h = lax.fori_loop(0, rows // SUBLANES, group, h_s[...])
    h_s[...] = h
    hl_ref[0] = h


def _rglru(p_r, row0, bsz, t, conv_state, w_conv, b_conv, w_ga, b_ga, w_gx, b_gx, sp, h0, o_init, *, rows):
    w = LRU_W
    nblk = t // rows
    assert t % rows == 0 and rows % SUBLANES == 0 and row0 % rows == 0
    rb0 = row0 // rows
    rows_spec = lambda col: pl.BlockSpec((rows, w), lambda ib, it: (rb0 + ib * nblk + it, col))
    full3 = lambda shape: pl.BlockSpec(shape, lambda ib, it: (0, 0, 0))
    full2 = lambda shape: pl.BlockSpec(shape, lambda ib, it: (0, 0))
    h_spec = pl.BlockSpec((1, 1, w), lambda ib, it: (ib, 0, 0))
    in_specs = [rows_spec(0), rows_spec(1),
                pl.BlockSpec((HALO, w), lambda ib, it: (
                    jnp.maximum((rb0 + ib * nblk + it) * (rows // HALO) - 1, 0), 0)),
                pl.BlockSpec((1, HALO, w), lambda ib, it: (ib, 0, 0)),
                full2((CONV_W, w)), full2((1, w)),
                full3((LRU_HEADS, LRU_BW, LRU_BW)), full3((LRU_HEADS, 1, LRU_BW)),
                full3((LRU_HEADS, LRU_BW, LRU_BW)), full3((LRU_HEADS, 1, LRU_BW)),
                full2((1, w)), h_spec]
    args = [p_r, p_r, p_r, conv_state, w_conv, b_conv, w_ga, b_ga, w_gx, b_gx, sp, h0]
    aliases = {}
    if o_init is not None:
        in_specs.append(pl.BlockSpec(memory_space=pl.ANY))
        args.append(o_init)
        aliases = {len(args) - 1: 0}
    vmem = 2 * (2 * rows * w * 4 + rows * w * 2 + 4 * LRU_HEADS * LRU_BW * LRU_BW * 4) + 2 * rows * w * 4 + (8 << 20)
    return pl.pallas_call(
        functools.partial(_lru_kernel, rows=rows),
        grid=(bsz, nblk),
        in_specs=in_specs,
        out_specs=[rows_spec(0), h_spec],
        out_shape=[jax.ShapeDtypeStruct((p_r.shape[0], w), BF16), jax.ShapeDtypeStruct((bsz, 1, w), F32)],
        scratch_shapes=[pltpu.VMEM((rows, w), F32), pltpu.VMEM((rows, w), F32), pltpu.VMEM((1, w), F32)],
        input_output_aliases=aliases,
        compiler_params=_params(2, vmem),
        name="rg_lru",
    )(*args)


def _merge_kernel(oa_ref, ob_ref, wa_ref, wb_ref, ga_ref, gb_ref, o_ref):
    ya = jnp.dot(oa_ref[...], wa_ref[...].astype(BF16), preferred_element_type=F32)
    yb = jnp.dot(ob_ref[...], wb_ref[...].astype(BF16), preferred_element_type=F32)
    o_ref[...] = (jax.nn.sigmoid(ga_ref[...]) * ya + jax.nn.sigmoid(gb_ref[...]) * yb).astype(o_ref.dtype)


def _merge(oa, ob, wa, wb, gates, ga_col0, gb_col0, *, tm, tn):
    m = oa.shape[0]
    n = wa.shape[1]
    assert m % tm == 0 and n % tn == 0 and ga_col0 % tn == 0 and gb_col0 % tn == 0
    ka, kb = oa.shape[1], ob.shape[1]
    vmem = 2 * (tm * (ka + kb) * 2 + (ka + kb) * tn * 4 + 2 * tm * tn * 4 + tm * tn * 2) + (ka + kb) * tn * 2
    row = lambda kdim: pl.BlockSpec((tm, kdim), lambda i, j: (i, 0))
    col = lambda kdim: pl.BlockSpec((kdim, tn), lambda i, j: (0, j))
    gate = lambda col0: pl.BlockSpec((tm, tn), lambda i, j: (i, col0 // tn + j))
    return pl.pallas_call(
        _merge_kernel,
        grid=(m // tm, n // tn),
        in_specs=[row(ka), row(kb), col(ka), col(kb), gate(ga_col0), gate(gb_col0)],
        out_specs=pl.BlockSpec((tm, tn), lambda i, j: (i, j)),
        out_shape=jax.ShapeDtypeStruct((m, n), BF16),
        compiler_params=_params(2, vmem),
        name="gated_merge",
    )(oa, ob, wa, wb, gates, gates)


def _pack_bf16_pairs(x):
    half = x.shape[1] // 2
    bits = lambda u: pltpu.bitcast(u.astype(BF16).astype(F32), jnp.uint32)
    return (bits(x[:, :half]) >> 16) | (bits(x[:, half:]) & jnp.uint32(0xFFFF0000))


def _unpack_bf16_pairs(words):
    lo = pltpu.bitcast(words << 16, F32)
    hi = pltpu.bitcast(words & jnp.uint32(0xFFFF0000), F32)
    return jnp.concatenate([lo, hi], axis=1).astype(BF16)


def _out_kernel(m_ref, x_ref, w_ref, g_ref, b_ref, wr_ref, br_ref, h_ref, hp_ref, lg_ref):
    y = jnp.dot(m_ref[...], w_ref[...].astype(BF16), preferred_element_type=F32)
    h = _layer_norm(DEEPNORM_ALPHA * x_ref[...] + y, g_ref[...], b_ref[...])
    h_ref[...] = h
    hp_ref[...] = _pack_bf16_pairs(h)
    split = lambda u: (u.astype(BF16), (u - u.astype(BF16).astype(F32)).astype(BF16))
    h_hi, h_lo = split(h)
    w_hi, w_lo = split(wr_ref[...])
    rows = h.shape[0]
    both = jnp.dot(jnp.concatenate([h_hi, h_lo], axis=0), w_hi, preferred_element_type=F32)
    lg_ref[...] = both[:rows] + both[rows:] + jnp.dot(h_hi, w_lo, preferred_element_type=F32) + br_ref[...]


def _out_proj(merged, x, w_out, g, b, w_router, b_router, *, tm):
    m, d = x.shape
    e = w_router.shape[1]
    assert m % tm == 0
    vmem = 2 * (tm * d * 2 + tm * d * 4 + d * d * 4 + tm * d * 6 + d * e * 4 + tm * 128 * 4) + d * d * 2
    rows = lambda width: pl.BlockSpec((tm, width), lambda i: (i, 0))
    full = lambda shape: pl.BlockSpec(shape, lambda i: (0, 0))
    return pl.pallas_call(
        _out_kernel,
        grid=(m // tm,),
        in_specs=[rows(d), rows(d), full((d, d)), full((1, d)), full((1, d)), full((d, e)), full((1, e))],
        out_specs=[rows(d), rows(d // 2), rows(e)],
        out_shape=[jax.ShapeDtypeStruct((m, d), F32), jax.ShapeDtypeStruct((m, d // 2), jnp.uint32),
                   jax.ShapeDtypeStruct((m, e), F32)],
        compiler_params=_params(1, vmem),
        name="out_proj_ln_router",
    )(merged, x, w_out, g, b, w_router, b_router)


def _moe_kernel(ge_ref, gn_ref, gofs_ref, tok_ref, h_hbm, wg_ref, wu_ref, bg_ref, bu_ref, wd_ref, bd_ref, o_ref,
                x_buf, sem, wg_s, wu_s, wd_s):
    ig = pl.program_id(0)
    jf = pl.program_id(1)
    n_valid = gn_ref[ig]
    slot = ig % 2

    def start_rows(g, buf_slot):
        base = gofs_ref[g]
        last = jnp.maximum(gn_ref[g] - 1, 0)
        n_fetch = sum(jnp.where(gn_ref[g] > row0, n_rows, 0) for row0, n_rows in MOE_SUB_BLOCKS)
        n_batches = n_fetch // DMA_ISSUE_UNROLL

        def body(ib, carry):
            for u in range(DMA_ISSUE_UNROLL):
                r = ib * DMA_ISSUE_UNROLL + u
                tok = tok_ref[base + jnp.minimum(r, last)]
                pltpu.make_async_copy(h_hbm.at[pl.ds(tok, 1), :], x_buf.at[buf_slot, pl.ds(r, 1), :],
                                      sem.at[buf_slot]).start(priority=u % 2)
            return carry
        lax.fori_loop(0, n_batches, body, 0)

    @pl.when(jf == 0)
    def _():
        @pl.when(ig == 0)
        def _():
            start_rows(0, 0)

        @pl.when(ig + 1 < pl.num_programs(0))
        def _():
            start_rows(ig + 1, (ig + 1) % 2)

        for row0, n_rows in MOE_SUB_BLOCKS:
            @pl.when(row0 < n_valid)
            def _():
                pltpu.make_async_copy(h_hbm.at[pl.ds(0, n_rows), :], x_buf.at[slot, pl.ds(row0, n_rows), :],
                                      sem.at[slot]).wait()

    @pl.when((n_valid == 0) & (jf == 0))
    def _():
        o_ref[...] = jnp.zeros(o_ref.shape, o_ref.dtype)

    @pl.when(n_valid > 0)
    def _():
        @pl.when(jf == 0)
        def _():
            o_ref[...] = jnp.broadcast_to(bd_ref[0], o_ref.shape)

        for sb, (row0, n_rows) in enumerate(MOE_SUB_BLOCKS):
            @pl.when(row0 < n_valid)
            def _():
                rows = slice(row0, row0 + n_rows)
                if sb == 0:
                    wg = wg_ref[0].astype(BF16)
                    wu = wu_ref[0].astype(BF16)
                    wd = wd_ref[0].astype(BF16)
                    wg_s[...] = wg
                    wu_s[...] = wu
                    wd_s[...] = wd
                else:
                    wg, wu, wd = wg_s[...], wu_s[...], wd_s[...]
                xb = _unpack_bf16_pairs(x_buf[slot, rows, :])
                hg = jnp.dot(xb, wg, preferred_element_type=F32) + bg_ref[0]
                hu = jnp.dot(xb, wu, preferred_element_type=F32) + bu_ref[0]
                hg = jnp.minimum(hg, SWIGLU_LIMIT)
                hu = jnp.clip(hu, -SWIGLU_LIMIT, SWIGLU_LIMIT)
                act = (hu + 1.0) * hg * jax.nn.sigmoid(SWIGLU_ALPHA * hg)
                o_ref[rows, :] += jnp.dot(act.astype(BF16), wd, preferred_element_type=F32)


def _moe_experts(h_packed, group_expert, group_valid, group_offset, sorted_tok, w_gu, b_gu, w_dn, b_dn):
    n_groups = group_expert.shape[0]
    d = 2 * h_packed.shape[1]
    nf = D_FF // MOE_FF_TILE
    gm, tf = MOE_GROUP_ROWS, MOE_FF_TILE

    def f_eff(ig, jf, gn):
        return jnp.where(gn[ig] > 0, jf, nf - 1)

    grid_spec = pltpu.PrefetchScalarGridSpec(
        num_scalar_prefetch=4,
        grid=(n_groups, nf),
        in_specs=[
            pl.BlockSpec(memory_space=pl.ANY),
            pl.BlockSpec((1, d, tf), lambda ig, jf, ge, gn, go, tk: (ge[ig], 0, f_eff(ig, jf, gn))),
            pl.BlockSpec((1, d, tf), lambda ig, jf, ge, gn, go, tk: (ge[ig], 0, nf + f_eff(ig, jf, gn))),
            pl.BlockSpec((1, 1, tf), lambda ig, jf, ge, gn, go, tk: (ge[ig], 0, f_eff(ig, jf, gn))),
            pl.BlockSpec((1, 1, tf), lambda ig, jf, ge, gn, go, tk: (ge[ig], 0, nf + f_eff(ig, jf, gn))),
            pl.BlockSpec((1, tf, d), lambda ig, jf, ge, gn, go, tk: (ge[ig], f_eff(ig, jf, gn), 0)),
            pl.BlockSpec((1, 1, d), lambda ig, jf, ge, gn, go, tk: (ge[ig], 0, 0)),
        ],
        out_specs=pl.BlockSpec((gm, d), lambda ig, jf, ge, gn, go, tk: (ig, 0)),
        scratch_shapes=[pltpu.VMEM((2, gm, d // 2), jnp.uint32), pltpu.SemaphoreType.DMA((2,)),
                        pltpu.VMEM((d, tf), BF16), pltpu.VMEM((d, tf), BF16), pltpu.VMEM((tf, d), BF16)],
    )
    vmem = 2 * gm * (d // 2) * 4 + 2 * (3 * d * tf * 4 + gm * d * 4) + 3 * d * tf * 2 + (6 << 20)
    return pl.pallas_call(
        _moe_kernel,
        grid_spec=grid_spec,
        out_shape=jax.ShapeDtypeStruct((n_groups * gm, d), F32),
        compiler_params=_params(2, vmem),
        name="moe_experts",
    )(group_expert, group_valid, group_offset, sorted_tok, h_packed, w_gu, w_gu, b_gu, b_gu, w_dn, b_dn)


def _combine_kernel(dest_ref, rows_hbm, gate_ref, h_ref, g_ref, b_ref, o_ref, buf, sem, *, tm):
    i = pl.program_id(0)
    n_steps = pl.num_programs(0)

    tokens_per_batch = DMA_ISSUE_UNROLL // TOP_K

    def start_rows(step, buf_slot):
        def body(ib, carry):
            for u in range(DMA_ISSUE_UNROLL):
                t = ib * tokens_per_batch + u // TOP_K
                kk = u % TOP_K
                src = dest_ref[(step * tm + t) * TOP_K + kk]
                pltpu.make_async_copy(rows_hbm.at[pl.ds(src, 1), :], buf.at[buf_slot, pl.ds(kk * tm + t, 1), :],
                                      sem.at[buf_slot]).start(priority=u % 2)
            return carry
        lax.fori_loop(0, tm // tokens_per_batch, body, 0)

    @pl.when(i == 0)
    def _():
        start_rows(0, 0)

    @pl.when(i + 1 < n_steps)
    def _():
        start_rows(i + 1, (i + 1) % 2)

    slot = i % 2
    pltpu.make_async_copy(rows_hbm.at[pl.ds(0, TOP_K * tm), :], buf.at[slot], sem.at[slot]).wait()
    y = sum(gate_ref[:, kk:kk + 1] * buf[slot, kk * tm:(kk + 1) * tm, :] for kk in range(TOP_K))
    o_ref[...] = _layer_norm(DEEPNORM_ALPHA * h_ref[...] + y, g_ref[...], b_ref[...])


def _combine(out_rows, dest, gate, h, g, b, *, tm):
    n, d = h.shape
    assert n % tm == 0
    grid_spec = pltpu.PrefetchScalarGridSpec(
        num_scalar_prefetch=1,
        grid=(n // tm,),
        in_specs=[pl.BlockSpec(memory_space=pl.ANY),
                  pl.BlockSpec((tm, TOP_K), lambda i, dest: (i, 0)),
                  pl.BlockSpec((tm, d), lambda i, dest: (i, 0)),
                  pl.BlockSpec((1, d), lambda i, dest: (0, 0)),
                  pl.BlockSpec((1, d), lambda i, dest: (0, 0))],
        out_specs=pl.BlockSpec((tm, d), lambda i, dest: (i, 0)),
        scratch_shapes=[pltpu.VMEM((2, TOP_K * tm, d), F32), pltpu.SemaphoreType.DMA((2,))],
    )
    vmem = 2 * TOP_K * tm * d * 4 + 2 * (2 * tm * d * 4 + tm * 128 * 4) + (4 << 20)
    return pl.pallas_call(
        functools.partial(_combine_kernel, tm=tm),
        grid_spec=grid_spec,
        out_shape=jax.ShapeDtypeStruct((n, d), F32),
        compiler_params=_params(1, vmem),
        name="moe_combine_ln",
    )(dest, out_rows, gate, h, g, b)


def _moe_ffn_ln(h, h_packed, logits, w_gu, b_gu, w_dn, b_dn, ln_g, ln_b):
    n, d = h.shape
    nk = n * TOP_K
    gm = MOE_GROUP_ROWS
    n_groups = N_EXPERTS + nk // gm
    top_logit, top_e = lax.top_k(logits, TOP_K)
    gate = jax.nn.softmax(top_logit, axis=-1)
    flat_e = top_e.reshape(nk)
    onehot = (flat_e[:, None] == jnp.arange(N_EXPERTS, dtype=flat_e.dtype)[None, :]).astype(jnp.int32)
    rank = jnp.take_along_axis(jnp.cumsum(onehot, axis=0), flat_e[:, None], axis=1)[:, 0] - 1
    counts = jnp.sum(onehot, axis=0)
    start_e = jnp.cumsum(counts) - counts
    groups_e = (counts + gm - 1) // gm
    gend_e = jnp.cumsum(groups_e)
    gstart_e = gend_e - groups_e
    dest = (gstart_e[flat_e] * gm + rank).astype(jnp.int32)
    sorted_tok = (jnp.argsort(flat_e, stable=True) // TOP_K).astype(jnp.int32)
    n_used = gend_e[-1]
    gidx = jnp.arange(n_groups, dtype=jnp.int32)
    gclamp = jnp.minimum(gidx, n_used - 1)
    group_expert = jnp.minimum(jnp.searchsorted(gend_e, gclamp, side="right"), N_EXPERTS - 1).astype(jnp.int32)
    within = gclamp - gstart_e[group_expert]
    used = gidx < n_used
    group_valid = jnp.where(used, jnp.clip(counts[group_expert] - within * gm, 0, gm), 0).astype(jnp.int32)
    group_offset = jnp.where(used, start_e[group_expert] + within * gm, 0).astype(jnp.int32)
    out_rows = _moe_experts(h_packed, group_expert, group_valid, group_offset, sorted_tok,
                            w_gu, b_gu.reshape(N_EXPERTS, 1, -1), w_dn, b_dn.reshape(N_EXPERTS, 1, -1))
    return _combine(out_rows, dest, gate, h, ln_g, ln_b, tm=COMBINE_ROWS)


def _pad_state(state):
    return jnp.pad(state, ((0, 0), (HALO - (CONV_W - 1), 0), (0, 0)))


def _last_rows(p, width, row0, bsz, t):
    if bsz == 1:
        return p[row0 + t - (CONV_W - 1):row0 + t, :width][None]
    return p[row0:row0 + bsz * t, :width].reshape(bsz, t, width)[:, t - (CONV_W - 1):]


def _dn_gates(p_ba, row0, bsz, t, chunk, a_log, dt_bias):
    ba = p_ba[row0:row0 + bsz * t, :2 * DN_V_HEADS].reshape(bsz, t, 2 * DN_V_HEADS)
    beta = jax.nn.sigmoid(ba[..., :DN_V_HEADS])
    g = -jnp.exp(a_log) * jax.nn.softplus(ba[..., DN_V_HEADS:] + dt_bias)
    n_ch = -(-t // chunk)

    def to5(u):
        u = jnp.pad(u, ((0, 0), (0, n_ch * chunk - t), (0, 0)))
        u = jnp.transpose(u, (0, 2, 1)).reshape(bsz, DN_QK_HEADS, 2, n_ch, chunk)
        return jnp.pad(u, ((0, 0),) * 4 + ((0, chunk),))

    return to5(g), to5(beta)


def kernel(x_prompt, x_sample, state_dn_conv, state_dn_ssm, state_lru_conv, state_lru_h, w_in, w_dn_conv, dn_a_log, dn_dt_bias, dn_norm_w, w_lru_conv, b_lru_conv, w_lru_ga, b_lru_ga, w_lru_gx, b_lru_gx, lru_lambda, w_proj_a, w_proj_b, w_out, ln1_g, ln1_b, w_router, b_router, w_gate_up, b_gate_up, w_down, b_down, ln2_g, ln2_b):
    assert w_in.shape[0] == 1, "single layer"
    bp, tp, d = x_prompt.shape
    bs, ts, _ = x_sample.shape
    np_, ns = bp * tp, bs * ts
    n = np_ + ns

    x_all = jnp.concatenate([x_prompt.reshape(np_, d), x_sample.reshape(ns, d)], axis=0)
    x_bf = x_all.astype(BF16)
    w_in0 = w_in[0]
    tm = n // 5
    p_a = _matmul(x_bf, w_in0, tm=tm, tn=1024, n_cols=OFF_B)
    p_ba = _matmul(x_bf, w_in0, tm=tm, tn=LANES, col0=OFF_B, n_cols=LANES)
    p_r = _matmul(x_bf, w_in0, tm=tm, tn=512, col0=OFF_LX)

    zeros = lambda *shape: jnp.zeros(shape, F32)
    norm_w = dn_norm_w[0].reshape(1, DN_HEAD)
    g5, beta5 = _dn_gates(p_ba, 0, bp, tp, GDN_CHUNK, dn_a_log[0], dn_dt_bias[0])
    o_a, dn_s_p = _gated_delta(
        p_a, 0, bp, tp, zeros(bp, HALO, DN_QKV_DIM), w_dn_conv[0], g5, beta5,
        zeros(bp, DN_V_HEADS, DN_HEAD, DN_HEAD), norm_w, jnp.zeros((n, DN_V_DIM), BF16),
        chunk=GDN_CHUNK, n_chunks=GDN_CHUNKS_PER_STEP, n_qk=GDN_QK_HEADS_PER_STEP)
    g5, beta5 = _dn_gates(p_ba, np_, bs, ts, GDN_CHUNK, dn_a_log[0], dn_dt_bias[0])
    o_a, dn_s_s = _gated_delta(
        p_a, np_, bs, ts, _pad_state(state_dn_conv[0]), w_dn_conv[0], g5, beta5, state_dn_ssm[0], norm_w, o_a,
        chunk=GDN_CHUNK, n_chunks=1, n_qk=4)

    lru_args = (w_lru_conv[0], b_lru_conv, w_lru_ga[0], b_lru_ga[0].reshape(LRU_HEADS, 1, LRU_BW),
                w_lru_gx[0], b_lru_gx[0].reshape(LRU_HEADS, 1, LRU_BW),
                jax.nn.softplus(-lru_lambda[0]).reshape(1, LRU_W))
    o_b, h_p = _rglru(p_r, 0, bp, tp, zeros(bp, HALO, LRU_W), *lru_args, zeros(bp, 1, LRU_W),
                      jnp.zeros((n, LRU_W), BF16), rows=256)
    o_b, h_s = _rglru(p_r, np_, bs, ts, _pad_state(state_lru_conv[0]), *lru_args,
                      state_lru_h[0].reshape(bs, 1, LRU_W), o_b, rows=ts)

    merged = _merge(o_a, o_b, w_proj_a[0], w_proj_b[0], p_r, 2 * LRU_W, 3 * LRU_W, tm=n // 10, tn=256)
    h, h_packed, logits = _out_proj(merged, x_all, w_out[0], ln1_g, ln1_b, w_router[0], b_router, tm=n // 26)
    out = _moe_ffn_ln(h, h_packed, logits, w_gate_up[0], b_gate_up[0], w_down[0], b_down[0], ln2_g, ln2_b)

    return (out[:np_].reshape(bp, tp, d), out[np_:].reshape(bs, ts, d),
            _last_rows(p_a, DN_QKV_DIM, 0, bp, tp)[None], dn_s_p[None],
            _last_rows(p_r, LRU_W, 0, bp, tp)[None], h_p.reshape(bp, LRU_W)[None],
            _last_rows(p_a, DN_QKV_DIM, np_, bs, ts)[None], dn_s_s[None],
            _last_rows(p_r, LRU_W, np_, bs, ts)[None], h_s.reshape(bs, LRU_W)[None])
```

```python
import functools

import jax
import jax.numpy as jnp
from jax import lax
from jax.experimental import pallas as pl
from jax.experimental.pallas import tpu as pltpu

F32 = jnp.float32
BF16 = jnp.bfloat16

D_MODEL = 2048
DN_QK_HEADS = 16
DN_V_HEADS = 32
DN_HEAD = 128
DN_QK_DIM = DN_QK_HEADS * DN_HEAD
DN_V_DIM = DN_V_HEADS * DN_HEAD
DN_QKV_DIM = 2 * DN_QK_DIM + DN_V_DIM
CONV_W = 4
LRU_W = D_MODEL
LRU_HEADS = 16
LRU_BW = LRU_W // LRU_HEADS
LRU_C = 8.0
N_EXPERTS = 32
TOP_K = 4
D_FF = D_MODEL
SWIGLU_LIMIT = 7.0
SWIGLU_ALPHA = 1.702
DEEPNORM_ALPHA = 2.0 ** 0.25
LN_EPS = 1e-5
NORM_EPS = 1e-6
OFF_Z = DN_QKV_DIM
OFF_B = OFF_Z + DN_V_DIM
OFF_A = OFF_B + DN_V_HEADS
OFF_LX = OFF_A + DN_V_HEADS
OFF_LY = OFF_LX + LRU_W
OFF_GA = OFF_LY + LRU_W
OFF_GB = OFF_GA + D_MODEL

V7X_VMEM_LIMIT_BYTES = 56 * 1024 * 1024
SUBLANES = 8
LANES = 128

HALO = SUBLANES
GDN_CHUNK = 64
GDN_CHUNKS_PER_STEP = 4
GDN_QK_HEADS_PER_STEP = 4
GDN_WAVE_CHAINS = 16
MOE_GROUP_ROWS = 1280
MOE_SUB_BLOCKS = ((512, MOE_GROUP_ROWS, 0, 1024), (0, 512, 0, 512),
                  (1024, MOE_GROUP_ROWS, 1024, 128), (1152, MOE_GROUP_ROWS, 1152, 128))
MOE_CHAIN_ROWS = 256
MOE_FF_TILE = 256
COMBINE_ROWS = 128
DMA_ISSUE_UNROLL = 8


def _params(n_axes, vmem_bytes):
    limit = min(V7X_VMEM_LIMIT_BYTES, int(vmem_bytes * 1.2) + (4 << 20))
    return pltpu.CompilerParams(dimension_semantics=("arbitrary",) * n_axes, vmem_limit_bytes=limit)


def _layer_norm(x, g, b):
    mu = jnp.mean(x, axis=-1, keepdims=True)
    xc = x - mu
    var = jnp.mean(xc * xc, axis=-1, keepdims=True)
    return xc * lax.rsqrt(var + LN_EPS) * g + b


def _causal_conv_rows(tail, x, w):
    r = x.shape[0]
    xx = jnp.concatenate([tail, x], axis=0)
    first = HALO - (CONV_W - 1)
    return sum(xx[first + j:first + j + r] * w[j:j + 1] for j in range(CONV_W))


def _dot_nt(a, b):
    return lax.dot_general(a, b, (((1,), (1,)), ((), ())), preferred_element_type=F32)


def _mm_kernel(x_ref, wt_ref, o_ref):
    o_ref[...] = _dot_nt(x_ref[...], wt_ref[...].astype(BF16)).astype(o_ref.dtype)


def _mm_shift_kernel(x_ref, wa_ref, wb_ref, o_ref, *, shift):
    wt = jnp.concatenate([wa_ref[shift:, :], wb_ref[:shift, :]], axis=0)
    o_ref[...] = _dot_nt(x_ref[...], wt.astype(BF16)).astype(o_ref.dtype)


def _matmul_nt(x_bf, w_t, *, tm, tn, col0=0, n_cols=None):
    m, k = x_bf.shape
    n = w_t.shape[0] - col0 if n_cols is None else n_cols
    shift = col0 % tn
    assert m % tm == 0 and n % tn == 0 and shift % SUBLANES == 0 and shift <= LANES
    vmem = 2 * (tm * k * 2 + k * (tn + LANES) * 4 + tm * tn * 4) + k * tn * 6
    x_spec = pl.BlockSpec((tm, k), lambda i, j: (i, 0))
    w_spec = pl.BlockSpec((tn, k), lambda i, j: (col0 // tn + j, 0))
    if shift == 0:
        body, w_specs, w_args = _mm_kernel, [w_spec], [w_t]
    else:
        body = functools.partial(_mm_shift_kernel, shift=shift)
        w_specs = [w_spec, pl.BlockSpec((LANES, k), lambda i, j: ((col0 // tn + j + 1) * (tn // LANES), 0))]
        w_args = [w_t, w_t]
    return pl.pallas_call(
        body,
        grid=(m // tm, n // tn),
        in_specs=[x_spec] + w_specs,
        out_specs=pl.BlockSpec((tm, tn), lambda i, j: (i, j)),
        out_shape=jax.ShapeDtypeStruct((m, n), F32),
        compiler_params=_params(2, vmem),
        name="dense_matmul",
    )(x_bf, *w_args)


def _l2norm(u):
    return u * lax.rsqrt(jnp.sum(u * u, axis=-1, keepdims=True) + NORM_EPS)


def _gdn_kernel(q_ref, k_ref, v_ref, z_ref, qh_ref, kh_ref, vh_ref, qs_ref, ks_ref, vs_ref, wq_ref, wk_ref, wv_ref,
                g_ref, beta_ref, s0_ref, nw_ref, *rest, chunk, n_chunks, n_qk):
    o_ref, s_ref, hist_ref = rest[-3:]
    c = chunk

    @pl.when(pl.program_id(2) == 0)
    def _():
        s_ref[...] = s0_ref[...]

    cw = 2 * c
    ri = lax.broadcasted_iota(jnp.int32, (c, cw), 0)
    ci = lax.broadcasted_iota(jnp.int32, (c, cw), 1)
    as_f = lambda m: jnp.where(m, 1.0, 0.0).astype(F32)
    causal_f, strict_f, eye_f = as_f(ri >= ci), as_f(ri > ci), as_f(ri == ci)
    le_f, hi_f, eye_hi = as_f(ri <= ci), as_f(ci >= c), as_f(ci == ri + c)
    n_square = c.bit_length() - 1
    nw = nw_ref[...]
    valid = min(c, q_ref.shape[0])
    zeros_c = lambda width: jnp.zeros((c, width), F32)

    def col_of(row):
        return jnp.sum(row * eye_f, axis=1, keepdims=True)

    def conv_silu(x_ref, halo_ref, state_ref, w_ref, ic, lanes, hist_col0):
        lo = ic * c
        if ic == 0:
            cols = slice(hist_col0 + lanes.start, hist_col0 + lanes.stop)
            hist_ref[0:HALO, cols] = jnp.where(pl.program_id(2) == 0, state_ref[0, :, lanes], halo_ref[:, lanes])
            hist_ref[HALO:HALO + valid, cols] = x_ref[0:valid, lanes]
            src, first, src_lanes = hist_ref, HALO - (CONV_W - 1), cols
        else:
            src, first, src_lanes = x_ref, lo - (CONV_W - 1), lanes
        y = sum(src[first + j:first + j + valid, src_lanes] * w_ref[j:j + 1, lanes] for j in range(CONV_W))
        y = y / (1.0 + jnp.exp(-y))
        return y if valid == c else jnp.concatenate([y, jnp.zeros((c - valid, y.shape[1]), F32)], axis=0)

    heads = range(2 * n_qk)
    wq = n_qk * DN_HEAD
    pre = {}
    wave_chunks = max(1, GDN_WAVE_CHAINS // (2 * n_qk))
    for w0 in range(0, n_chunks, wave_chunks):
        ics = range(w0, min(w0 + wave_chunks, n_chunks))
        qkeys = [(ic, j) for ic in ics for j in range(n_qk)]
        keys = [(ic, hv) for ic in ics for hv in heads]
        q, k = {}, {}
        for ic, j in qkeys:
            qk_lanes = slice(j * DN_HEAD, (j + 1) * DN_HEAD)
            q[ic, j] = _l2norm(conv_silu(q_ref, qh_ref, qs_ref, wq_ref, ic, qk_lanes, 0)) * (DN_HEAD ** -0.5)
            k[ic, j] = _l2norm(conv_silu(k_ref, kh_ref, ks_ref, wk_ref, ic, qk_lanes, wq))
        kq = {}
        for key in qkeys:
            k_t = jnp.concatenate([k[key], zeros_c(DN_HEAD)], axis=0).T.astype(BF16)
            kq[key] = jnp.dot(jnp.concatenate([k[key], q[key]], axis=0).astype(BF16), k_t,
                              preferred_element_type=F32)
        st, pt = {}, {}
        for ic, hv in keys:
            j, hh = divmod(hv, 2)
            chunk_row = pl.ds(pl.program_id(2) * n_chunks + ic, 1)
            g_row = g_ref[0, j, hh, chunk_row, :]
            beta_col = col_of(beta_ref[0, j, hh, chunk_row, :])
            gc_col = jnp.sum(g_row * causal_f, axis=1, keepdims=True)
            gc_row = jnp.sum(col_of(g_row) * le_f, axis=0, keepdims=True)
            decay = jnp.exp(jnp.minimum(gc_col - gc_row, 0.0)) * causal_f
            st[ic, hv] = dict(beta_col=beta_col, gc_col=gc_col, egc=jnp.exp(gc_col), g_last=gc_col[c - 1:c, :],
                              attn=(kq[ic, j][c:] * decay).astype(BF16))
            pt[ic, hv] = eye_hi - (beta_col * kq[ic, j][:c] * decay) * strict_f
        for _ in range(n_square):
            for key in keys:
                cur = pt[key]
                pt[key] = jnp.dot(cur.astype(BF16), jnp.concatenate([cur, zeros_c(cw)], axis=0).astype(BF16),
                                  preferred_element_type=F32) + cur * hi_f
        sol = {}
        for ic, hv in keys:
            a = st[ic, hv]
            v = conv_silu(v_ref, vh_ref, vs_ref, wv_ref, ic, slice(hv * DN_HEAD, (hv + 1) * DN_HEAD), 2 * wq)
            rhs = jnp.concatenate([v * a["beta_col"], k[ic, hv // 2] * (a["beta_col"] * a["egc"])], axis=1)
            rhs = jnp.concatenate([zeros_c(2 * DN_HEAD), rhs], axis=0)
            sol[ic, hv] = jnp.dot(pt[ic, hv].astype(BF16), rhs.astype(BF16), preferred_element_type=F32)
        for ic, hv in keys:
            a = st[ic, hv]
            kh = k[ic, hv // 2]
            pre[ic, hv] = dict(
                u_base=sol[ic, hv][:, :DN_HEAD],
                kc_qd=jnp.concatenate([sol[ic, hv][:, DN_HEAD:], q[ic, hv // 2] * a["egc"]], axis=0).astype(BF16),
                attn=a["attn"],
                k_dec_t=(kh * jnp.exp(a["g_last"] - a["gc_col"])).T.astype(BF16),
                s_decay=jnp.exp(a["g_last"]))

    s = {hv: s_ref[0, hv] for hv in heads}
    for ic in range(n_chunks):
        ks_qs = {hv: jnp.dot(pre[ic, hv]["kc_qd"], s[hv].astype(BF16), preferred_element_type=F32) for hv in heads}
        u = {hv: pre[ic, hv]["u_base"] - ks_qs[hv][:c] for hv in heads}
        o = {hv: ks_qs[hv][c:] + jnp.dot(pre[ic, hv]["attn"],
                                         jnp.concatenate([u[hv], zeros_c(DN_HEAD)], axis=0).astype(BF16),
                                         preferred_element_type=F32) for hv in heads}
        for hv in heads:
            s[hv] = s[hv] * pre[ic, hv]["s_decay"] + jnp.dot(pre[ic, hv]["k_dec_t"], u[hv].astype(BF16),
                                                           preferred_element_type=F32)
        for hv in heads:
            lanes = slice(hv * DN_HEAD, (hv + 1) * DN_HEAD)
            on = o[hv] * lax.rsqrt(jnp.mean(o[hv] * o[hv], axis=-1, keepdims=True) + NORM_EPS) * nw
            z = z_ref[ic * c:ic * c + valid, lanes]
            o_ref[ic * c:ic * c + valid, lanes] = (on[:valid] * (z * jax.nn.sigmoid(z))).astype(o_ref.dtype)
    for hv in heads:
        s_ref[0, hv] = s[hv]


def _gated_delta(p_a, row0, bsz, t, conv_state, w_conv, g5, beta5, s0, norm_w, o_init, *, chunk, n_chunks, n_qk):
    rows = min(t, chunk * n_chunks)
    nblk = t // rows
    assert t % rows == 0 and row0 % rows == 0 and (rows == chunk * n_chunks or n_chunks == 1)
    assert rows % HALO == 0
    rb0 = row0 // rows
    wq, wv = DN_HEAD * n_qk, 2 * DN_HEAD * n_qk
    kq0, kv0, kz0 = DN_QK_DIM // wq, 2 * DN_QK_DIM // wv, OFF_Z // wv

    def rows_spec(width, col0):
        return pl.BlockSpec((rows, width), lambda ib, ih, it: (rb0 + ib * nblk + it, col0 + ih))

    def halo_spec(width, col0):
        return pl.BlockSpec((HALO, width), lambda ib, ih, it: (
            jnp.maximum((rb0 + ib * nblk + it) * (rows // HALO) - 1, 0), col0 + ih))

    def state_spec(width, col0):
        return pl.BlockSpec((1, HALO, width), lambda ib, ih, it: (ib, 0, col0 + ih))

    def w_spec(width, col0):
        return pl.BlockSpec((CONV_W, width), lambda ib, ih, it: (0, col0 + ih))

    g_spec = pl.BlockSpec((1, n_qk, 2) + g5.shape[3:], lambda ib, ih, it: (ib, ih, 0, 0, 0))
    s_spec = pl.BlockSpec((1, 2 * n_qk, DN_HEAD, DN_HEAD), lambda ib, ih, it: (ib, ih, 0, 0))
    in_specs = [rows_spec(wq, 0), rows_spec(wq, kq0), rows_spec(wv, kv0), rows_spec(wv, kz0),
                halo_spec(wq, 0), halo_spec(wq, kq0), halo_spec(wv, kv0),
                state_spec(wq, 0), state_spec(wq, kq0), state_spec(wv, kv0),
                w_spec(wq, 0), w_spec(wq, kq0), w_spec(wv, kv0),
                g_spec, g_spec, s_spec, pl.BlockSpec((1, DN_HEAD), lambda ib, ih, it: (0, 0))]
    args = [p_a, p_a, p_a, p_a, p_a, p_a, p_a, conv_state, conv_state, conv_state,
            w_conv, w_conv, w_conv, g5, beta5, s0, norm_w]
    aliases = {}
    if o_init is not None:
        in_specs.append(pl.BlockSpec(memory_space=pl.ANY))
        args.append(o_init)
        aliases = {len(args) - 1: 0}
    vmem = 2 * (2 * rows * wq * 4 + 2 * rows * wv * 4 + rows * wv * 2 + 4 * n_qk * DN_HEAD * DN_HEAD * 4) + (16 << 20)
    return pl.pallas_call(
        functools.partial(_gdn_kernel, chunk=chunk, n_chunks=n_chunks, n_qk=n_qk),
        grid=(bsz, DN_QK_HEADS // n_qk, nblk),
        in_specs=in_specs,
        out_specs=[rows_spec(wv, 0), s_spec],
        out_shape=[jax.ShapeDtypeStruct((p_a.shape[0], DN_V_DIM), BF16),
                   jax.ShapeDtypeStruct((bsz, DN_V_HEADS, DN_HEAD, DN_HEAD), F32)],
        scratch_shapes=[pltpu.VMEM((HALO + min(chunk, rows), 2 * wq + wv), F32)],
        input_output_aliases=aliases,
        compiler_params=_params(3, vmem),
        name="gated_delta_rule",
    )(*args)


def _lru_kernel(x_ref, y_ref, halo_ref, state_ref, wc_ref, bc_ref, wga_ref, bga_ref, wgx_ref, bgx_ref, sp_ref, h0_ref,
                *rest, rows):
    o_ref, hl_ref, a_s, b_s, h_s = rest[-5:]

    @pl.when(pl.program_id(1) == 0)
    def _():
        h_s[...] = h0_ref[0]

    for hd in range(LRU_HEADS):
        lanes = slice(hd * LRU_BW, (hd + 1) * LRU_BW)
        tail = jnp.where(pl.program_id(1) == 0, state_ref[0, :, lanes], halo_ref[:, lanes])
        xh = _causal_conv_rows(tail, x_ref[:, lanes], wc_ref[:, lanes]) + bc_ref[:, lanes]
        xh_bf = xh.astype(BF16)
        r = jax.nn.sigmoid(jnp.dot(xh_bf, wga_ref[hd].astype(BF16), preferred_element_type=F32) + bga_ref[hd])
        i = jax.nn.sigmoid(jnp.dot(xh_bf, wgx_ref[hd].astype(BF16), preferred_element_type=F32) + bgx_ref[hd])
        log_a = (-LRU_C) * r * sp_ref[:, lanes]
        a = jnp.exp(log_a)
        a_s[:, lanes] = a
        b_s[:, lanes] = jnp.sqrt(1.0 - a * a) * (i * xh)

    row = lax.broadcasted_iota(jnp.int32, (SUBLANES, LRU_W), 0)

    def group(ig, h):
        r0 = pl.multiple_of(ig * SUBLANES, SUBLANES)
        a = a_s[pl.ds(r0, SUBLANES), :]
        b = b_s[pl.ds(r0, SUBLANES), :]
        for sh in (1, 2, 4):
            keep = row >= sh
            a_sh = jnp.where(keep, pltpu.roll(a, sh, axis=0), 1.0)
            b_sh = jnp.where(keep, pltpu.roll(b, sh, axis=0), 0.0)
            b = a * b_sh + b
            a = a * a_sh
        hs = a * h + b
        gy = jax.nn.gelu(y_ref[pl.ds(r0, SUBLANES), :])
        o_ref[pl.ds(r0, SUBLANES), :] = (hs * gy).astype(o_ref.dtype)
        return hs[SUBLANES - 1:SUBLANES, :]

    h = lax.fori_loop(0, rows // SUBLANES, group, h_s[...])
    h_s[...] = h
    hl_ref[0] = h


def _rglru(p_r, row0, bsz, t, conv_state, w_conv, b_conv, w_ga, b_ga, w_gx, b_gx, sp, h0, o_init, *, rows):
    w = LRU_W
    nblk = t // rows
    assert t % rows == 0 and rows % SUBLANES == 0 and row0 % rows == 0
    rb0 = row0 // rows
    rows_spec = lambda col: pl.BlockSpec((rows, w), lambda ib, it: (rb0 + ib * nblk + it, col))
    full3 = lambda shape: pl.BlockSpec(shape, lambda ib, it: (0, 0, 0))
    full2 = lambda shape: pl.BlockSpec(shape, lambda ib, it: (0, 0))
    h_spec = pl.BlockSpec((1, 1, w), lambda ib, it: (ib, 0, 0))
    in_specs = [rows_spec(0), rows_spec(1),
                pl.BlockSpec((HALO, w), lambda ib, it: (
                    jnp.maximum((rb0 + ib * nblk + it) * (rows // HALO) - 1, 0), 0)),
                pl.BlockSpec((1, HALO, w), lambda ib, it: (ib, 0, 0)),
                full2((CONV_W, w)), full2((1, w)),
                full3((LRU_HEADS, LRU_BW, LRU_BW)), full3((LRU_HEADS, 1, LRU_BW)),
                full3((LRU_HEADS, LRU_BW, LRU_BW)), full3((LRU_HEADS, 1, LRU_BW)),
                full2((1, w)), h_spec]
    args = [p_r, p_r, p_r, conv_state, w_conv, b_conv, w_ga, b_ga, w_gx, b_gx, sp, h0]
    aliases = {}
    if o_init is not None:
        in_specs.append(pl.BlockSpec(memory_space=pl.ANY))
        args.append(o_init)
        aliases = {len(args) - 1: 0}
    vmem = 2 * (2 * rows * w * 4 + rows * w * 2 + 4 * LRU_HEADS * LRU_BW * LRU_BW * 4) + 2 * rows * w * 4 + (8 << 20)
    return pl.pallas_call(
        functools.partial(_lru_kernel, rows=rows),
        grid=(bsz, nblk),
        in_specs=in_specs,
        out_specs=[rows_spec(0), h_spec],
        out_shape=[jax.ShapeDtypeStruct((p_r.shape[0], w), BF16), jax.ShapeDtypeStruct((bsz, 1, w), F32)],
        scratch_shapes=[pltpu.VMEM((rows, w), F32), pltpu.VMEM((rows, w), F32), pltpu.VMEM((1, w), F32)],
        input_output_aliases=aliases,
        compiler_params=_params(2, vmem),
        name="rg_lru",
    )(*args)


def _merge_kernel(oa_ref, ob_ref, wa_ref, wb_ref, ga_ref, gb_ref, o_ref):
    ya = jnp.dot(oa_ref[...], wa_ref[...].astype(BF16), preferred_element_type=F32)
    yb = jnp.dot(ob_ref[...], wb_ref[...].astype(BF16), preferred_element_type=F32)
    o_ref[...] = (jax.nn.sigmoid(ga_ref[...]) * ya + jax.nn.sigmoid(gb_ref[...]) * yb).astype(o_ref.dtype)


def _merge(oa, ob, wa, wb, gates, ga_col0, gb_col0, *, tm, tn):
    m = oa.shape[0]
    n = wa.shape[1]
    assert m % tm == 0 and n % tn == 0 and ga_col0 % tn == 0 and gb_col0 % tn == 0
    ka, kb = oa.shape[1], ob.shape[1]
    vmem = 2 * (tm * (ka + kb) * 2 + (ka + kb) * tn * 4 + 2 * tm * tn * 4 + tm * tn * 2) + (ka + kb) * tn * 2
    row = lambda kdim: pl.BlockSpec((tm, kdim), lambda i, j: (i, 0))
    col = lambda kdim: pl.BlockSpec((kdim, tn), lambda i, j: (0, j))
    gate = lambda col0: pl.BlockSpec((tm, tn), lambda i, j: (i, col0 // tn + j))
    return pl.pallas_call(
        _merge_kernel,
        grid=(m // tm, n // tn),
        in_specs=[row(ka), row(kb), col(ka), col(kb), gate(ga_col0), gate(gb_col0)],
        out_specs=pl.BlockSpec((tm, tn), lambda i, j: (i, j)),
        out_shape=jax.ShapeDtypeStruct((m, n), BF16),
        compiler_params=_params(2, vmem),
        name="gated_merge",
    )(oa, ob, wa, wb, gates, gates)


def _pack_bf16_pairs(x):
    half = x.shape[1] // 2
    bits = lambda u: pltpu.bitcast(u.astype(BF16).astype(F32), jnp.uint32)
    return (bits(x[:, :half]) >> 16) | (bits(x[:, half:]) & jnp.uint32(0xFFFF0000))


def _unpack_bf16_pairs(words):
    lo = pltpu.bitcast(words << 16, F32)
    hi = pltpu.bitcast(words & jnp.uint32(0xFFFF0000), F32)
    return jnp.concatenate([lo, hi], axis=1).astype(BF16)


def _out_kernel(m_ref, x_ref, w_ref, g_ref, b_ref, wr_ref, br_ref, h_ref, hp_ref, lg_ref):
    y = jnp.dot(m_ref[...], w_ref[...].astype(BF16), preferred_element_type=F32)
    h = _layer_norm(DEEPNORM_ALPHA * x_ref[...] + y, g_ref[...], b_ref[...])
    h_ref[...] = h
    hp_ref[...] = _pack_bf16_pairs(h)
    split = lambda u: (u.astype(BF16), (u - u.astype(BF16).astype(F32)).astype(BF16))
    h_hi, h_lo = split(h)
    w_hi, w_lo = split(wr_ref[...])
    rows = h.shape[0]
    both = jnp.dot(jnp.concatenate([h_hi, h_lo], axis=0), w_hi, preferred_element_type=F32)
    lg_ref[...] = both[:rows] + both[rows:] + jnp.dot(h_hi, w_lo, preferred_element_type=F32) + br_ref[...]


def _out_proj(merged, x, w_out, g, b, w_router, b_router, *, tm):
    m, d = x.shape
    e = w_router.shape[1]
    assert m % tm == 0
    vmem = 2 * (tm * d * 2 + tm * d * 4 + d * d * 4 + tm * d * 6 + d * e * 4 + tm * 128 * 4) + d * d * 2
    rows = lambda width: pl.BlockSpec((tm, width), lambda i: (i, 0))
    full = lambda shape: pl.BlockSpec(shape, lambda i: (0, 0))
    return pl.pallas_call(
        _out_kernel,
        grid=(m // tm,),
        in_specs=[rows(d), rows(d), full((d, d)), full((1, d)), full((1, d)), full((d, e)), full((1, e))],
        out_specs=[rows(d), rows(d // 2), rows(e)],
        out_shape=[jax.ShapeDtypeStruct((m, d), F32), jax.ShapeDtypeStruct((m, d // 2), jnp.uint32),
                   jax.ShapeDtypeStruct((m, e), F32)],
        compiler_params=_params(1, vmem),
        name="out_proj_ln_router",
    )(merged, x, w_out, g, b, w_router, b_router)


def _sub_block_runs(n_valid, lo, hi):
    return (n_valid > lo) & (n_valid <= hi)


def _moe_kernel(ge_ref, gn_ref, gofs_ref, tok_ref, h_hbm, wg_ref, wu_ref, bg_ref, bu_ref, wd_ref, bd_ref, o_ref,
                x_buf, sem, wg_s, wu_s, wd_s):
    ig = pl.program_id(0)
    jf = pl.program_id(1)
    n_valid = gn_ref[ig]
    slot = ig % 2

    n_groups = pl.num_programs(0)
    rows_per_step = MOE_GROUP_ROWS // pl.num_programs(1)

    def start_row(g, buf_slot, r, priority):
        tok = tok_ref[gofs_ref[g] + jnp.minimum(r, jnp.maximum(gn_ref[g] - 1, 0))]
        pltpu.make_async_copy(h_hbm.at[pl.ds(tok, 1), :], x_buf.at[buf_slot, pl.ds(r, 1), :],
                              sem.at[buf_slot]).start(priority=priority)

    def wait_rows(buf_slot):
        pltpu.make_async_copy(h_hbm.at[pl.ds(0, MOE_GROUP_ROWS), :], x_buf.at[buf_slot], sem.at[buf_slot]).wait()

    @pl.when(jf == 0)
    def _():
        @pl.when(ig == 0)
        def _():
            def body(ib, carry):
                for u in range(DMA_ISSUE_UNROLL):
                    start_row(0, 0, ib * DMA_ISSUE_UNROLL + u, u % 2)
                return carry
            lax.fori_loop(0, MOE_GROUP_ROWS // DMA_ISSUE_UNROLL, body, 0)

        @pl.when((ig == 0) | (gn_ref[jnp.maximum(ig - 1, 0)] > 0))
        def _():
            wait_rows(slot)

    @pl.when((ig == n_groups - 1) & (jf == pl.num_programs(1) - 1) & (n_valid > 0))
    def _():
        wait_rows(1 - slot)

    @pl.when((n_valid == 0) & (jf == 0))
    def _():
        o_ref[...] = jnp.zeros(o_ref.shape, o_ref.dtype)

    @pl.when(n_valid > 0)
    def _():
        @pl.when(jf == 0)
        def _():
            o_ref[...] = jnp.broadcast_to(bd_ref[0], o_ref.shape)

        for lo, hi, row0, n_rows in MOE_SUB_BLOCKS:
            @pl.when(_sub_block_runs(n_valid, lo, hi))
            def _():
                if row0 == 0:
                    wg = wg_ref[0].astype(BF16)
                    wu = wu_ref[0].astype(BF16)
                    wd = wd_ref[0].astype(BF16)
                    wg_s[...] = wg
                    wu_s[...] = wu
                    wd_s[...] = wd
                else:
                    wg, wu, wd = wg_s[...], wu_s[...], wd_s[...]
                step = min(MOE_CHAIN_ROWS, n_rows)
                chains = [slice(r, r + step) for r in range(row0, row0 + n_rows, step)]
                xb = [_unpack_bf16_pairs(x_buf[slot, rows, :]) for rows in chains]
                hg = [jnp.dot(x, wg, preferred_element_type=F32) + bg_ref[0] for x in xb]
                hu = [jnp.dot(x, wu, preferred_element_type=F32) + bu_ref[0] for x in xb]
                act = []
                for g, u in zip(hg, hu):
                    g = jnp.minimum(g, SWIGLU_LIMIT)
                    u = jnp.clip(u, -SWIGLU_LIMIT, SWIGLU_LIMIT)
                    act.append(((u + 1.0) * g * jax.nn.sigmoid(SWIGLU_ALPHA * g)).astype(BF16))
                for rows, a in zip(chains, act):
                    o_ref[rows, :] += jnp.dot(a, wd, preferred_element_type=F32)
                if row0 == 0:
                    g_next = jnp.minimum(ig + 1, n_groups - 1)
                    for u in range(MOE_GROUP_ROWS // (D_FF // MOE_FF_TILE)):
                        start_row(g_next, 1 - slot, jf * rows_per_step + u, u % 2)


def _moe_experts(h_packed, group_expert, group_valid, group_offset, sorted_tok, w_gu, b_gu, w_dn, b_dn):
    n_groups = group_expert.shape[0]
    d = 2 * h_packed.shape[1]
    nf = D_FF // MOE_FF_TILE
    gm, tf = MOE_GROUP_ROWS, MOE_FF_TILE

    def f_eff(ig, jf, gn):
        return jnp.where(gn[ig] > 0, jf, nf - 1)

    grid_spec = pltpu.PrefetchScalarGridSpec(
        num_scalar_prefetch=4,
        grid=(n_groups, nf),
        in_specs=[
            pl.BlockSpec(memory_space=pl.ANY),
            pl.BlockSpec((1, d, tf), lambda ig, jf, ge, gn, go, tk: (ge[ig], 0, f_eff(ig, jf, gn))),
            pl.BlockSpec((1, d, tf), lambda ig, jf, ge, gn, go, tk: (ge[ig], 0, nf + f_eff(ig, jf, gn))),
            pl.BlockSpec((1, 1, tf), lambda ig, jf, ge, gn, go, tk: (ge[ig], 0, f_eff(ig, jf, gn))),
            pl.BlockSpec((1, 1, tf), lambda ig, jf, ge, gn, go, tk: (ge[ig], 0, nf + f_eff(ig, jf, gn))),
            pl.BlockSpec((1, tf, d), lambda ig, jf, ge, gn, go, tk: (ge[ig], f_eff(ig, jf, gn), 0)),
            pl.BlockSpec((1, 1, d), lambda ig, jf, ge, gn, go, tk: (ge[ig], 0, 0)),
        ],
        out_specs=pl.BlockSpec((gm, d), lambda ig, jf, ge, gn, go, tk: (ig, 0)),
        scratch_shapes=[pltpu.VMEM((2, gm, d // 2), jnp.uint32), pltpu.SemaphoreType.DMA((2,)),
                        pltpu.VMEM((d, tf), BF16), pltpu.VMEM((d, tf), BF16), pltpu.VMEM((tf, d), BF16)],
    )
    vmem = 2 * gm * (d // 2) * 4 + 2 * (3 * d * tf * 4 + gm * d * 4) + 3 * d * tf * 2 + (6 << 20)
    return pl.pallas_call(
        _moe_kernel,
        grid_spec=grid_spec,
        out_shape=jax.ShapeDtypeStruct((n_groups * gm, d), F32),
        compiler_params=_params(2, vmem),
        name="moe_experts",
    )(group_expert, group_valid, group_offset, sorted_tok, h_packed, w_gu, w_gu, b_gu, b_gu, w_dn, b_dn)


def _combine_kernel(dest_ref, rows_hbm, gate_ref, h_ref, g_ref, b_ref, o_ref, buf, sem, *, tm):
    i = pl.program_id(0)
    n_steps = pl.num_programs(0)

    tokens_per_batch = DMA_ISSUE_UNROLL // TOP_K

    def start_rows(step, buf_slot):
        def body(ib, carry):
            for u in range(DMA_ISSUE_UNROLL):
                t = ib * tokens_per_batch + u // TOP_K
                kk = u % TOP_K
                src = dest_ref[(step * tm + t) * TOP_K + kk]
                pltpu.make_async_copy(rows_hbm.at[pl.ds(src, 1), :], buf.at[buf_slot, pl.ds(kk * tm + t, 1), :],
                                      sem.at[buf_slot]).start(priority=u % 2)
            return carry
        lax.fori_loop(0, tm // tokens_per_batch, body, 0)

    @pl.when(i == 0)
    def _():
        start_rows(0, 0)

    @pl.when(i + 1 < n_steps)
    def _():
        start_rows(i + 1, (i + 1) % 2)

    slot = i % 2
    pltpu.make_async_copy(rows_hbm.at[pl.ds(0, TOP_K * tm), :], buf.at[slot], sem.at[slot]).wait()
    y = sum(gate_ref[:, kk:kk + 1] * buf[slot, kk * tm:(kk + 1) * tm, :] for kk in range(TOP_K))
    o_ref[...] = _layer_norm(DEEPNORM_ALPHA * h_ref[...] + y, g_ref[...], b_ref[...])


def _combine(out_rows, dest, gate, h, g, b, *, tm):
    n, d = h.shape
    assert n % tm == 0
    grid_spec = pltpu.PrefetchScalarGridSpec(
        num_scalar_prefetch=1,
        grid=(n // tm,),
        in_specs=[pl.BlockSpec(memory_space=pl.ANY),
                  pl.BlockSpec((tm, TOP_K), lambda i, dest: (i, 0)),
                  pl.BlockSpec((tm, d), lambda i, dest: (i, 0)),
                  pl.BlockSpec((1, d), lambda i, dest: (0, 0)),
                  pl.BlockSpec((1, d), lambda i, dest: (0, 0))],
        out_specs=pl.BlockSpec((tm, d), lambda i, dest: (i, 0)),
        scratch_shapes=[pltpu.VMEM((2, TOP_K * tm, d), F32), pltpu.SemaphoreType.DMA((2,))],
    )
    vmem = 2 * TOP_K * tm * d * 4 + 2 * (2 * tm * d * 4 + tm * 128 * 4) + (4 << 20)
    return pl.pallas_call(
        functools.partial(_combine_kernel, tm=tm),
        grid_spec=grid_spec,
        out_shape=jax.ShapeDtypeStruct((n, d), F32),
        compiler_params=_params(1, vmem),
        name="moe_combine_ln",
    )(dest, out_rows, gate, h, g, b)


def _moe_ffn_ln(h, h_packed, logits, w_gu, b_gu, w_dn, b_dn, ln_g, ln_b):
    n, d = h.shape
    nk = n * TOP_K
    gm = MOE_GROUP_ROWS
    n_groups = N_EXPERTS + nk // gm
    top_logit, top_e = lax.top_k(logits, TOP_K)
    gate = jax.nn.softmax(top_logit, axis=-1)
    flat_e = top_e.reshape(nk)
    onehot = (flat_e[:, None] == jnp.arange(N_EXPERTS, dtype=flat_e.dtype)[None, :]).astype(jnp.int32)
    rank = jnp.take_along_axis(jnp.cumsum(onehot, axis=0), flat_e[:, None], axis=1)[:, 0] - 1
    counts = jnp.sum(onehot, axis=0)
    start_e = jnp.cumsum(counts) - counts
    groups_e = (counts + gm - 1) // gm
    gend_e = jnp.cumsum(groups_e)
    gstart_e = gend_e - groups_e
    dest = (gstart_e[flat_e] * gm + rank).astype(jnp.int32)
    sorted_tok = (jnp.argsort(flat_e, stable=True) // TOP_K).astype(jnp.int32)
    n_used = gend_e[-1]
    gidx = jnp.arange(n_groups, dtype=jnp.int32)
    gclamp = jnp.minimum(gidx, n_used - 1)
    group_expert = jnp.minimum(jnp.searchsorted(gend_e, gclamp, side="right"), N_EXPERTS - 1).astype(jnp.int32)
    within = gclamp - gstart_e[group_expert]
    used = gidx < n_used
    group_valid = jnp.where(used, jnp.clip(counts[group_expert] - within * gm, 0, gm), 0).astype(jnp.int32)
    group_offset = jnp.where(used, start_e[group_expert] + within * gm, 0).astype(jnp.int32)
    out_rows = _moe_experts(h_packed, group_expert, group_valid, group_offset, sorted_tok,
                            w_gu, b_gu.reshape(N_EXPERTS, 1, -1), w_dn, b_dn.reshape(N_EXPERTS, 1, -1))
    return _combine(out_rows, dest, gate, h, ln_g, ln_b, tm=COMBINE_ROWS)


def _pad_state(state):
    return jnp.pad(state, ((0, 0), (HALO - (CONV_W - 1), 0), (0, 0)))


def _last_rows(p, width, row0, bsz, t):
    if bsz == 1:
        return p[row0 + t - (CONV_W - 1):row0 + t, :width][None]
    return p[row0:row0 + bsz * t, :width].reshape(bsz, t, width)[:, t - (CONV_W - 1):]


def _dn_gates(p_ba, row0, bsz, t, chunk, a_log, dt_bias):
    ba = p_ba[row0:row0 + bsz * t, :2 * DN_V_HEADS].reshape(bsz, t, 2 * DN_V_HEADS)
    beta = jax.nn.sigmoid(ba[..., :DN_V_HEADS])
    g = -jnp.exp(a_log) * jax.nn.softplus(ba[..., DN_V_HEADS:] + dt_bias)
    n_ch = -(-t // chunk)

    def to5(u):
        u = jnp.pad(u, ((0, 0), (0, n_ch * chunk - t), (0, 0)))
        u = jnp.transpose(u, (0, 2, 1)).reshape(bsz, DN_QK_HEADS, 2, n_ch, chunk)
        return jnp.pad(u, ((0, 0),) * 4 + ((0, chunk),))

    return to5(g), to5(beta)


def kernel(x_prompt, x_sample, state_dn_conv, state_dn_ssm, state_lru_conv, state_lru_h, w_in, w_dn_conv, dn_a_log, dn_dt_bias, dn_norm_w, w_lru_conv, b_lru_conv, w_lru_ga, b_lru_ga, w_lru_gx, b_lru_gx, lru_lambda, w_proj_a, w_proj_b, w_out, ln1_g, ln1_b, w_router, b_router, w_gate_up, b_gate_up, w_down, b_down, ln2_g, ln2_b):
    assert w_in.shape[0] == 1, "single layer"
    bp, tp, d = x_prompt.shape
    bs, ts, _ = x_sample.shape
    np_, ns = bp * tp, bs * ts
    n = np_ + ns

    x_all = jnp.concatenate([x_prompt.reshape(np_, d), x_sample.reshape(ns, d)], axis=0)
    x_bf = x_all.astype(BF16)
    w_in_t = jnp.swapaxes(w_in[0], 0, 1)
    tm = n // 5
    p_a = _matmul_nt(x_bf, w_in_t, tm=tm, tn=1024, n_cols=OFF_B)
    p_ba = _matmul_nt(x_bf, w_in_t, tm=tm, tn=LANES, col0=OFF_B, n_cols=LANES)
    p_r = _matmul_nt(x_bf, w_in_t, tm=tm, tn=1024, col0=OFF_LX)

    zeros = lambda *shape: jnp.zeros(shape, F32)
    norm_w = dn_norm_w[0].reshape(1, DN_HEAD)
    g5, beta5 = _dn_gates(p_ba, 0, bp, tp, GDN_CHUNK, dn_a_log[0], dn_dt_bias[0])
    o_a, dn_s_p = _gated_delta(
        p_a, 0, bp, tp, zeros(bp, HALO, DN_QKV_DIM), w_dn_conv[0], g5, beta5,
        zeros(bp, DN_V_HEADS, DN_HEAD, DN_HEAD), norm_w, jnp.zeros((n, DN_V_DIM), BF16),
        chunk=GDN_CHUNK, n_chunks=GDN_CHUNKS_PER_STEP, n_qk=GDN_QK_HEADS_PER_STEP)
    g5, beta5 = _dn_gates(p_ba, np_, bs, ts, GDN_CHUNK, dn_a_log[0], dn_dt_bias[0])
    o_a, dn_s_s = _gated_delta(
        p_a, np_, bs, ts, _pad_state(state_dn_conv[0]), w_dn_conv[0], g5, beta5, state_dn_ssm[0], norm_w, o_a,
        chunk=GDN_CHUNK, n_chunks=1, n_qk=4)

    lru_args = (w_lru_conv[0], b_lru_conv, w_lru_ga[0], b_lru_ga[0].reshape(LRU_HEADS, 1, LRU_BW),
                w_lru_gx[0], b_lru_gx[0].reshape(LRU_HEADS, 1, LRU_BW),
                jax.nn.softplus(-lru_lambda[0]).reshape(1, LRU_W))
    o_b, h_p = _rglru(p_r, 0, bp, tp, zeros(bp, HALO, LRU_W), *lru_args, zeros(bp, 1, LRU_W),
                      jnp.zeros((n, LRU_W), BF16), rows=256)
    o_b, h_s = _rglru(p_r, np_, bs, ts, _pad_state(state_lru_conv[0]), *lru_args,
                      state_lru_h[0].reshape(bs, 1, LRU_W), o_b, rows=ts)

    merged = _merge(o_a, o_b, w_proj_a[0], w_proj_b[0], p_r, 2 * LRU_W, 3 * LRU_W, tm=n // 10, tn=256)
    h, h_packed, logits = _out_proj(merged, x_all, w_out[0], ln1_g, ln1_b, w_router[0], b_router, tm=n // 26)
    out = _moe_ffn_ln(h, h_packed, logits, w_gate_up[0], b_gate_up[0], w_down[0], b_down[0], ln2_g, ln2_b)

    return (out[:np_].reshape(bp, tp, d), out[np_:].reshape(bs, ts, d),
            _last_rows(p_a, DN_QKV_DIM, 0, bp, tp)[None], dn_s_p[None],
            _last_rows(p_r, LRU_W, 0, bp, tp)[None], h_p.reshape(bp, LRU_W)[None],
            _last_rows(p_a, DN_QKV_DIM, np_, bs, ts)[None], dn_s_s[None],
            _last_rows(p_r, LRU_W, np_, bs, ts)[None], h_s.reshape(bs, LRU_W)[None])
```

```python
import functools

import jax
import jax.numpy as jnp
from jax import lax
from jax.experimental import pallas as pl
from jax.experimental.pallas import tpu as pltpu

F32 = jnp.float32
BF16 = jnp.bfloat16

D_MODEL = 2048
DN_QK_HEADS = 16
DN_V_HEADS = 32
DN_HEAD = 128
DN_QK_DIM = DN_QK_HEADS * DN_HEAD
DN_V_DIM = DN_V_HEADS * DN_HEAD
DN_QKV_DIM = 2 * DN_QK_DIM + DN_V_DIM
CONV_W = 4
LRU_W = D_MODEL
LRU_HEADS = 16
LRU_BW = LRU_W // LRU_HEADS
LRU_C = 8.0
N_EXPERTS = 32
TOP_K = 4
D_FF = D_MODEL
SWIGLU_LIMIT = 7.0
SWIGLU_ALPHA = 1.702
DEEPNORM_ALPHA = 2.0 ** 0.25
LN_EPS = 1e-5
NORM_EPS = 1e-6
OFF_Z = DN_QKV_DIM
OFF_B = OFF_Z + DN_V_DIM
OFF_A = OFF_B + DN_V_HEADS
OFF_LX = OFF_A + DN_V_HEADS
OFF_LY = OFF_LX + LRU_W
OFF_GA = OFF_LY + LRU_W
OFF_GB = OFF_GA + D_MODEL

V7X_VMEM_LIMIT_BYTES = 56 * 1024 * 1024
SUBLANES = 8
LANES = 128

HALO = SUBLANES
GDN_CHUNK = 64
GDN_CHUNKS_PER_STEP = 4
GDN_QK_HEADS_PER_STEP = 4
GDN_WAVE_CHAINS = 32
MOE_GROUP_ROWS = 1280
MOE_SUB_BLOCKS = ((512, MOE_GROUP_ROWS, 0, 1024), (0, 512, 0, 512),
                  (1024, MOE_GROUP_ROWS, 1024, 128), (1152, MOE_GROUP_ROWS, 1152, 128))
MOE_CHAIN_ROWS = 256
MOE_FF_TILE = 256
COMBINE_ROWS = 128
DMA_ISSUE_UNROLL = 8


def _params(n_axes, vmem_bytes):
    limit = min(V7X_VMEM_LIMIT_BYTES, int(vmem_bytes * 1.2) + (4 << 20))
    return pltpu.CompilerParams(dimension_semantics=("arbitrary",) * n_axes, vmem_limit_bytes=limit)


def _layer_norm(x, g, b):
    mu = jnp.mean(x, axis=-1, keepdims=True)
    xc = x - mu
    var = jnp.mean(xc * xc, axis=-1, keepdims=True)
    return xc * lax.rsqrt(var + LN_EPS) * g + b


def _causal_conv_rows(tail, x, w):
    r = x.shape[0]
    xx = jnp.concatenate([tail, x], axis=0)
    first = HALO - (CONV_W - 1)
    return sum(xx[first + j:first + j + r] * w[j:j + 1] for j in range(CONV_W))


def _dot_nt(a, b):
    return lax.dot_general(a, b, (((1,), (1,)), ((), ())), preferred_element_type=F32)


def _mm_kernel(x_ref, wt_ref, o_ref):
    o_ref[...] = _dot_nt(x_ref[...], wt_ref[...].astype(BF16)).astype(o_ref.dtype)


def _mm_shift_kernel(x_ref, wa_ref, wb_ref, o_ref, *, shift):
    wt = jnp.concatenate([wa_ref[shift:, :], wb_ref[:shift, :]], axis=0)
    o_ref[...] = _dot_nt(x_ref[...], wt.astype(BF16)).astype(o_ref.dtype)


def _matmul_nt(x_bf, w_t, *, tm, tn, col0=0, n_cols=None):
    m, k = x_bf.shape
    n = w_t.shape[0] - col0 if n_cols is None else n_cols
    shift = col0 % tn
    assert m % tm == 0 and n % tn == 0 and shift % SUBLANES == 0 and shift <= LANES
    vmem = 2 * (tm * k * 2 + k * (tn + LANES) * 4 + tm * tn * 4) + k * tn * 6
    x_spec = pl.BlockSpec((tm, k), lambda i, j: (i, 0))
    w_spec = pl.BlockSpec((tn, k), lambda i, j: (col0 // tn + j, 0))
    if shift == 0:
        body, w_specs, w_args = _mm_kernel, [w_spec], [w_t]
    else:
        body = functools.partial(_mm_shift_kernel, shift=shift)
        w_specs = [w_spec, pl.BlockSpec((LANES, k), lambda i, j: ((col0 // tn + j + 1) * (tn // LANES), 0))]
        w_args = [w_t, w_t]
    return pl.pallas_call(
        body,
        grid=(m // tm, n // tn),
        in_specs=[x_spec] + w_specs,
        out_specs=pl.BlockSpec((tm, tn), lambda i, j: (i, j)),
        out_shape=jax.ShapeDtypeStruct((m, n), F32),
        compiler_params=_params(2, vmem),
        name="dense_matmul",
    )(x_bf, *w_args)


def _l2norm(u):
    return u * lax.rsqrt(jnp.sum(u * u, axis=-1, keepdims=True) + NORM_EPS)


def _gdn_kernel(q_ref, k_ref, v_ref, z_ref, qh_ref, kh_ref, vh_ref, qs_ref, ks_ref, vs_ref, wq_ref, wk_ref, wv_ref,
                g_ref, beta_ref, s0_ref, nw_ref, *rest, chunk, n_chunks, n_qk):
    o_ref, s_ref, hist_ref = rest[-3:]
    c = chunk

    @pl.when(pl.program_id(2) == 0)
    def _():
        s_ref[...] = s0_ref[...]

    cw = 2 * c
    ri = lax.broadcasted_iota(jnp.int32, (c, cw), 0)
    ci = lax.broadcasted_iota(jnp.int32, (c, cw), 1)
    as_f = lambda m: jnp.where(m, 1.0, 0.0).astype(F32)
    causal_f, strict_f, eye_f = as_f(ri >= ci), as_f(ri > ci), as_f(ri == ci)
    hi_f, eye_hi = as_f(ci >= c), as_f(ci == ri + c)
    half_lane = jnp.where(ci[:SUBLANES] >= c, ci[:SUBLANES] - c, ci[:SUBLANES])
    n_square = c.bit_length() - 1
    nw = nw_ref[...]
    valid = min(c, q_ref.shape[0])
    zeros_c = lambda width: jnp.zeros((c, width), F32)

    def col_of(row):
        return jnp.sum(row * eye_f, axis=1, keepdims=True)

    def conv_silu(x_ref, halo_ref, state_ref, w_ref, ic, lanes, hist_col0):
        lo = ic * c
        if ic == 0:
            cols = slice(hist_col0 + lanes.start, hist_col0 + lanes.stop)
            hist_ref[0:HALO, cols] = jnp.where(pl.program_id(2) == 0, state_ref[0, :, lanes], halo_ref[:, lanes])
            hist_ref[HALO:HALO + valid, cols] = x_ref[0:valid, lanes]
            src, first, src_lanes = hist_ref, HALO - (CONV_W - 1), cols
        else:
            src, first, src_lanes = x_ref, lo - (CONV_W - 1), lanes
        y = sum(src[first + j:first + j + valid, src_lanes] * w_ref[j:j + 1, lanes] for j in range(CONV_W))
        y = y / (1.0 + jnp.exp(-y))
        return y if valid == c else jnp.concatenate([y, jnp.zeros((c - valid, y.shape[1]), F32)], axis=0)

    heads = range(2 * n_qk)
    wq = n_qk * DN_HEAD
    pre = {}
    wave_chunks = max(1, GDN_WAVE_CHAINS // (2 * n_qk))
    for w0 in range(0, n_chunks, wave_chunks):
        ics = range(w0, min(w0 + wave_chunks, n_chunks))
        qkeys = [(ic, j) for ic in ics for j in range(n_qk)]
        keys = [(ic, hv) for ic in ics for hv in heads]
        q, k = {}, {}
        for ic, j in qkeys:
            qk_lanes = slice(j * DN_HEAD, (j + 1) * DN_HEAD)
            q[ic, j] = _l2norm(conv_silu(q_ref, qh_ref, qs_ref, wq_ref, ic, qk_lanes, 0)) * (DN_HEAD ** -0.5)
            k[ic, j] = _l2norm(conv_silu(k_ref, kh_ref, ks_ref, wk_ref, ic, qk_lanes, wq))
        kq, k_t = {}, {}
        for key in qkeys:
            k_t[key] = jnp.concatenate([k[key], zeros_c(DN_HEAD)], axis=0).T
            kq[key] = jnp.dot(jnp.concatenate([k[key], q[key]], axis=0).astype(BF16), k_t[key].astype(BF16),
                              preferred_element_type=F32)
        st, pt = {}, {}
        chunk_row = lambda ic: pl.ds(pl.program_id(2) * n_chunks + ic, 1)
        g_rows = jnp.concatenate([g_ref[0, hv // 2, hv % 2, chunk_row(ic), :] for ic, hv in keys], axis=0)
        gc_rows = g_rows
        for sh in [1 << b for b in range(c.bit_length() - 1)]:
            gc_rows = gc_rows + pltpu.roll(gc_rows, sh, axis=1) * as_f(half_lane[:1] >= sh)
        egc_rows = jnp.exp(gc_rows)
        for ik, (ic, hv) in enumerate(keys):
            j, hh = divmod(hv, 2)
            g_row = g_rows[ik:ik + 1]
            beta_row = beta_ref[0, j, hh, chunk_row(ic), :]
            beta_col = col_of(beta_row)
            gc_col = jnp.sum(g_row * causal_f, axis=1, keepdims=True)
            gc_row = gc_rows[ik:ik + 1]
            egc_row = egc_rows[ik:ik + 1]
            g_last = gc_row[:, c - 1:c]
            decay = jnp.exp(jnp.minimum(gc_col - gc_row, 0.0)) * causal_f
            st[ic, hv] = dict(
                t_scale=hi_f * beta_row, egc_row=egc_row, s_decay=jnp.exp(g_last),
                attn=(kq[ic, j][c:] * decay + eye_hi * egc_row).astype(BF16),
                k_dec_t=(k_t[ic, j] * jnp.exp(g_last - gc_row)).astype(BF16))
            pt[ic, hv] = eye_hi - (beta_col * kq[ic, j][:c] * decay) * strict_f
        for _ in range(n_square):
            for key in keys:
                cur = pt[key]
                pt[key] = jnp.dot(cur.astype(BF16), jnp.concatenate([cur, zeros_c(cw)], axis=0).astype(BF16),
                                  preferred_element_type=F32) + cur * hi_f
        u_base, k_cum = {}, {}
        for ic, hv in keys:
            a = st[ic, hv]
            v = conv_silu(v_ref, vh_ref, vs_ref, wv_ref, ic, slice(hv * DN_HEAD, (hv + 1) * DN_HEAD), 2 * wq)
            t_b = pt[ic, hv] * a["t_scale"]
            below = lambda m: jnp.concatenate([zeros_c(DN_HEAD), m], axis=0).astype(BF16)
            u_base[ic, hv] = jnp.dot(t_b.astype(BF16), below(v), preferred_element_type=F32)
            k_cum[ic, hv] = jnp.dot((t_b * a["egc_row"]).astype(BF16), below(k[ic, hv // 2]),
                                    preferred_element_type=F32)
        for ic, hv in keys:
            a = st[ic, hv]
            pre[ic, hv] = dict(u_base=u_base[ic, hv], attn=a["attn"], k_dec_t=a["k_dec_t"], s_decay=a["s_decay"],
                               kc_q=jnp.concatenate([k_cum[ic, hv], q[ic, hv // 2]], axis=0).astype(BF16))

    s = {hv: s_ref[0, hv] for hv in heads}
    for ic in range(n_chunks):
        ks_qs = {hv: jnp.dot(pre[ic, hv]["kc_q"], s[hv].astype(BF16), preferred_element_type=F32) for hv in heads}
        u = {hv: pre[ic, hv]["u_base"] - ks_qs[hv][:c] for hv in heads}
        o = {hv: jnp.dot(pre[ic, hv]["attn"], jnp.concatenate([u[hv], ks_qs[hv][c:]], axis=0).astype(BF16),
                         preferred_element_type=F32) for hv in heads}
        for hv in heads:
            s[hv] = s[hv] * pre[ic, hv]["s_decay"] + jnp.dot(
                pre[ic, hv]["k_dec_t"], jnp.concatenate([u[hv], zeros_c(DN_HEAD)], axis=0).astype(BF16),
                preferred_element_type=F32)
        for hv in heads:
            lanes = slice(hv * DN_HEAD, (hv + 1) * DN_HEAD)
            on = o[hv] * lax.rsqrt(jnp.mean(o[hv] * o[hv], axis=-1, keepdims=True) + NORM_EPS) * nw
            z = z_ref[ic * c:ic * c + valid, lanes]
            o_ref[ic * c:ic * c + valid, lanes] = (on[:valid] * (z * jax.nn.sigmoid(z))).astype(o_ref.dtype)
    for hv in heads:
        s_ref[0, hv] = s[hv]


def _gated_delta(p_a, row0, bsz, t, conv_state, w_conv, g5, beta5, s0, norm_w, o_init, *, chunk, n_chunks, n_qk):
    rows = min(t, chunk * n_chunks)
    nblk = t // rows
    assert t % rows == 0 and row0 % rows == 0 and (rows == chunk * n_chunks or n_chunks == 1)
    assert rows % HALO == 0
    rb0 = row0 // rows
    wq, wv = DN_HEAD * n_qk, 2 * DN_HEAD * n_qk
    kq0, kv0, kz0 = DN_QK_DIM // wq, 2 * DN_QK_DIM // wv, OFF_Z // wv

    def rows_spec(width, col0):
        return pl.BlockSpec((rows, width), lambda ib, ih, it: (rb0 + ib * nblk + it, col0 + ih))

    def halo_spec(width, col0):
        return pl.BlockSpec((HALO, width), lambda ib, ih, it: (
            jnp.maximum((rb0 + ib * nblk + it) * (rows // HALO) - 1, 0), col0 + ih))

    def state_spec(width, col0):
        return pl.BlockSpec((1, HALO, width), lambda ib, ih, it: (ib, 0, col0 + ih))

    def w_spec(width, col0):
        return pl.BlockSpec((CONV_W, width), lambda ib, ih, it: (0, col0 + ih))

    g_spec = pl.BlockSpec((1, n_qk, 2) + g5.shape[3:], lambda ib, ih, it: (ib, ih, 0, 0, 0))
    s_spec = pl.BlockSpec((1, 2 * n_qk, DN_HEAD, DN_HEAD), lambda ib, ih, it: (ib, ih, 0, 0))
    in_specs = [rows_spec(wq, 0), rows_spec(wq, kq0), rows_spec(wv, kv0), rows_spec(wv, kz0),
                halo_spec(wq, 0), halo_spec(wq, kq0), halo_spec(wv, kv0),
                state_spec(wq, 0), state_spec(wq, kq0), state_spec(wv, kv0),
                w_spec(wq, 0), w_spec(wq, kq0), w_spec(wv, kv0),
                g_spec, g_spec, s_spec, pl.BlockSpec((1, DN_HEAD), lambda ib, ih, it: (0, 0))]
    args = [p_a, p_a, p_a, p_a, p_a, p_a, p_a, conv_state, conv_state, conv_state,
            w_conv, w_conv, w_conv, g5, beta5, s0, norm_w]
    aliases = {}
    if o_init is not None:
        in_specs.append(pl.BlockSpec(memory_space=pl.ANY))
        args.append(o_init)
        aliases = {len(args) - 1: 0}
    vmem = 2 * (2 * rows * wq * 4 + 2 * rows * wv * 4 + rows * wv * 2 + 4 * n_qk * DN_HEAD * DN_HEAD * 4) + (16 << 20)
    return pl.pallas_call(
        functools.partial(_gdn_kernel, chunk=chunk, n_chunks=n_chunks, n_qk=n_qk),
        grid=(bsz, DN_QK_HEADS // n_qk, nblk),
        in_specs=in_specs,
        out_specs=[rows_spec(wv, 0), s_spec],
        out_shape=[jax.ShapeDtypeStruct((p_a.shape[0], DN_V_DIM), BF16),
                   jax.ShapeDtypeStruct((bsz, DN_V_HEADS, DN_HEAD, DN_HEAD), F32)],
        scratch_shapes=[pltpu.VMEM((HALO + min(chunk, rows), 2 * wq + wv), F32)],
        input_output_aliases=aliases,
        compiler_params=_params(3, vmem),
        name="gated_delta_rule",
    )(*args)


def _lru_kernel(x_ref, y_ref, halo_ref, state_ref, wc_ref, bc_ref, wga_ref, bga_ref, wgx_ref, bgx_ref, sp_ref, h0_ref,
                *rest, rows):
    o_ref, hl_ref, a_s, b_s, h_s = rest[-5:]

    @pl.when(pl.program_id(1) == 0)
    def _():
        h_s[...] = h0_ref[0]

    for hd in range(LRU_HEADS):
        lanes = slice(hd * LRU_BW, (hd + 1) * LRU_BW)
        tail = jnp.where(pl.program_id(1) == 0, state_ref[0, :, lanes], halo_ref[:, lanes])
        xh = _causal_conv_rows(tail, x_ref[:, lanes], wc_ref[:, lanes]) + bc_ref[:, lanes]
        xh_bf = xh.astype(BF16)
        r = jax.nn.sigmoid(jnp.dot(xh_bf, wga_ref[hd].astype(BF16), preferred_element_type=F32) + bga_ref[hd])
        i = jax.nn.sigmoid(jnp.dot(xh_bf, wgx_ref[hd].astype(BF16), preferred_element_type=F32) + bgx_ref[hd])
        log_a = (-LRU_C) * r * sp_ref[:, lanes]
        a = jnp.exp(log_a)
        a_s[:, lanes] = a
        b_s[:, lanes] = jnp.sqrt(1.0 - a * a) * (i * xh)

    row = lax.broadcasted_iota(jnp.int32, (SUBLANES, LRU_W), 0)

    def group(ig, h):
        r0 = pl.multiple_of(ig * SUBLANES, SUBLANES)
        a = a_s[pl.ds(r0, SUBLANES), :]
        b = b_s[pl.ds(r0, SUBLANES), :]
        for sh in (1, 2, 4):
            keep = row >= sh
            a_sh = jnp.where(keep, pltpu.roll(a, sh, axis=0), 1.0)
            b_sh = jnp.where(keep, pltpu.roll(b, sh, axis=0), 0.0)
            b = a * b_sh + b
            a = a * a_sh
        hs = a * h + b
        gy = jax.nn.gelu(y_ref[pl.ds(r0, SUBLANES), :])
        o_ref[pl.ds(r0, SUBLANES), :] = (hs * gy).astype(o_ref.dtype)
        return hs[SUBLANES - 1:SUBLANES, :]

    h = lax.fori_loop(0, rows // SUBLANES, group, h_s[...])
    h_s[...] = h
    hl_ref[0] = h


def _rglru(p_r, row0, bsz, t, conv_state, w_conv, b_conv, w_ga, b_ga, w_gx, b_gx, sp, h0, o_init, *, rows):
    w = LRU_W
    nblk = t // rows
    assert t % rows == 0 and rows % SUBLANES == 0 and row0 % rows == 0
    rb0 = row0 // rows
    rows_spec = lambda col: pl.BlockSpec((rows, w), lambda ib, it: (rb0 + ib * nblk + it, col))
    full3 = lambda shape: pl.BlockSpec(shape, lambda ib, it: (0, 0, 0))
    full2 = lambda shape: pl.BlockSpec(shape, lambda ib, it: (0, 0))
    h_spec = pl.BlockSpec((1, 1, w), lambda ib, it: (ib, 0, 0))
    in_specs = [rows_spec(0), rows_spec(1),
                pl.BlockSpec((HALO, w), lambda ib, it: (
                    jnp.maximum((rb0 + ib * nblk + it) * (rows // HALO) - 1, 0), 0)),
                pl.BlockSpec((1, HALO, w), lambda ib, it: (ib, 0, 0)),
                full2((CONV_W, w)), full2((1, w)),
                full3((LRU_HEADS, LRU_BW, LRU_BW)), full3((LRU_HEADS, 1, LRU_BW)),
                full3((LRU_HEADS, LRU_BW, LRU_BW)), full3((LRU_HEADS, 1, LRU_BW)),
                full2((1, w)), h_spec]
    args = [p_r, p_r, p_r, conv_state, w_conv, b_conv, w_ga, b_ga, w_gx, b_gx, sp, h0]
    aliases = {}
    if o_init is not None:
        in_specs.append(pl.BlockSpec(memory_space=pl.ANY))
        args.append(o_init)
        aliases = {len(args) - 1: 0}
    vmem = 2 * (2 * rows * w * 4 + rows * w * 2 + 4 * LRU_HEADS * LRU_BW * LRU_BW * 4) + 2 * rows * w * 4 + (8 << 20)
    return pl.pallas_call(
        functools.partial(_lru_kernel, rows=rows),
        grid=(bsz, nblk),
        in_specs=in_specs,
        out_specs=[rows_spec(0), h_spec],
        out_shape=[jax.ShapeDtypeStruct((p_r.shape[0], w), BF16), jax.ShapeDtypeStruct((bsz, 1, w), F32)],
        scratch_shapes=[pltpu.VMEM((rows, w), F32), pltpu.VMEM((rows, w), F32), pltpu.VMEM((1, w), F32)],
        input_output_aliases=aliases,
        compiler_params=_params(2, vmem),
        name="rg_lru",
    )(*args)


def _merge_kernel(oa_ref, ob_ref, wa_ref, wb_ref, ga_ref, gb_ref, o_ref):
    ya = jnp.dot(oa_ref[...], wa_ref[...].astype(BF16), preferred_element_type=F32)
    yb = jnp.dot(ob_ref[...], wb_ref[...].astype(BF16), preferred_element_type=F32)
    o_ref[...] = (jax.nn.sigmoid(ga_ref[...]) * ya + jax.nn.sigmoid(gb_ref[...]) * yb).astype(o_ref.dtype)


def _merge(oa, ob, wa, wb, gates, ga_col0, gb_col0, *, tm, tn):
    m = oa.shape[0]
    n = wa.shape[1]
    assert m % tm == 0 and n % tn == 0 and ga_col0 % tn == 0 and gb_col0 % tn == 0
    ka, kb = oa.shape[1], ob.shape[1]
    vmem = 2 * (tm * (ka + kb) * 2 + (ka + kb) * tn * 4 + 2 * tm * tn * 4 + tm * tn * 2) + (ka + kb) * tn * 2
    row = lambda kdim: pl.BlockSpec((tm, kdim), lambda i, j: (i, 0))
    col = lambda kdim: pl.BlockSpec((kdim, tn), lambda i, j: (0, j))
    gate = lambda col0: pl.BlockSpec((tm, tn), lambda i, j: (i, col0 // tn + j))
    return pl.pallas_call(
        _merge_kernel,
        grid=(m // tm, n // tn),
        in_specs=[row(ka), row(kb), col(ka), col(kb), gate(ga_col0), gate(gb_col0)],
        out_specs=pl.BlockSpec((tm, tn), lambda i, j: (i, j)),
        out_shape=jax.ShapeDtypeStruct((m, n), BF16),
        compiler_params=_params(2, vmem),
        name="gated_merge",
    )(oa, ob, wa, wb, gates, gates)


def _pack_bf16_pairs(x):
    half = x.shape[1] // 2
    bits = lambda u: pltpu.bitcast(u.astype(BF16).astype(F32), jnp.uint32)
    return (bits(x[:, :half]) >> 16) | (bits(x[:, half:]) & jnp.uint32(0xFFFF0000))


def _unpack_bf16_pairs(words):
    lo = pltpu.bitcast(words << 16, F32)
    hi = pltpu.bitcast(words & jnp.uint32(0xFFFF0000), F32)
    return jnp.concatenate([lo, hi], axis=1).astype(BF16)


def _out_kernel(m_ref, x_ref, w_ref, g_ref, b_ref, wr_ref, br_ref, h_ref, hp_ref, lg_ref):
    y = jnp.dot(m_ref[...], w_ref[...].astype(BF16), preferred_element_type=F32)
    h = _layer_norm(DEEPNORM_ALPHA * x_ref[...] + y, g_ref[...], b_ref[...])
    h_ref[...] = h
    hp_ref[...] = _pack_bf16_pairs(h)
    split = lambda u: (u.astype(BF16), (u - u.astype(BF16).astype(F32)).astype(BF16))
    h_hi, h_lo = split(h)
    w_hi, w_lo = split(wr_ref[...])
    rows = h.shape[0]
    both = jnp.dot(jnp.concatenate([h_hi, h_lo], axis=0), w_hi, preferred_element_type=F32)
    lg_ref[...] = both[:rows] + both[rows:] + jnp.dot(h_hi, w_lo, preferred_element_type=F32) + br_ref[...]


def _out_proj(merged, x, w_out, g, b, w_router, b_router, *, tm):
    m, d = x.shape
    e = w_router.shape[1]
    assert m % tm == 0
    vmem = 2 * (tm * d * 2 + tm * d * 4 + d * d * 4 + tm * d * 6 + d * e * 4 + tm * 128 * 4) + d * d * 2
    rows = lambda width: pl.BlockSpec((tm, width), lambda i: (i, 0))
    full = lambda shape: pl.BlockSpec(shape, lambda i: (0, 0))
    return pl.pallas_call(
        _out_kernel,
        grid=(m // tm,),
        in_specs=[rows(d), rows(d), full((d, d)), full((1, d)), full((1, d)), full((d, e)), full((1, e))],
        out_specs=[rows(d), rows(d // 2), rows(e)],
        out_shape=[jax.ShapeDtypeStruct((m, d), F32), jax.ShapeDtypeStruct((m, d // 2), jnp.uint32),
                   jax.ShapeDtypeStruct((m, e), F32)],
        compiler_params=_params(1, vmem),
        name="out_proj_ln_router",
    )(merged, x, w_out, g, b, w_router, b_router)


def _sub_block_runs(n_valid, lo, hi):
    return (n_valid > lo) & (n_valid <= hi)


def _moe_kernel(ge_ref, gn_ref, gofs_ref, tok_ref, h_hbm, wg_ref, wu_ref, bg_ref, bu_ref, wd_ref, bd_ref, o_ref,
                x_buf, sem, wg_s, wu_s, wd_s):
    ig = pl.program_id(0)
    jf = pl.program_id(1)
    n_valid = gn_ref[ig]
    slot = ig % 2

    n_groups = pl.num_programs(0)
    rows_per_step = MOE_GROUP_ROWS // pl.num_programs(1)

    def start_row(g, buf_slot, r, priority):
        tok = tok_ref[gofs_ref[g] + jnp.minimum(r, jnp.maximum(gn_ref[g] - 1, 0))]
        pltpu.make_async_copy(h_hbm.at[pl.ds(tok, 1), :], x_buf.at[buf_slot, pl.ds(r, 1), :],
                              sem.at[buf_slot]).start(priority=priority)

    def wait_rows(buf_slot):
        pltpu.make_async_copy(h_hbm.at[pl.ds(0, MOE_GROUP_ROWS), :], x_buf.at[buf_slot], sem.at[buf_slot]).wait()

    @pl.when(jf == 0)
    def _():
        @pl.when(ig == 0)
        def _():
            def body(ib, carry):
                for u in range(DMA_ISSUE_UNROLL):
                    start_row(0, 0, ib * DMA_ISSUE_UNROLL + u, u % 2)
                return carry
            lax.fori_loop(0, MOE_GROUP_ROWS // DMA_ISSUE_UNROLL, body, 0)

        @pl.when((ig == 0) | (gn_ref[jnp.maximum(ig - 1, 0)] > 0))
        def _():
            wait_rows(slot)

    @pl.when((ig == n_groups - 1) & (jf == pl.num_programs(1) - 1) & (n_valid > 0))
    def _():
        wait_rows(1 - slot)

    @pl.when((n_valid == 0) & (jf == 0))
    def _():
        o_ref[...] = jnp.zeros(o_ref.shape, o_ref.dtype)

    @pl.when(n_valid > 0)
    def _():
        @pl.when(jf == 0)
        def _():
            o_ref[...] = jnp.broadcast_to(bd_ref[0], o_ref.shape)

        for lo, hi, row0, n_rows in MOE_SUB_BLOCKS:
            @pl.when(_sub_block_runs(n_valid, lo, hi))
            def _():
                if row0 == 0:
                    wg = wg_ref[0].astype(BF16)
                    wu = wu_ref[0].astype(BF16)
                    wd = wd_ref[0].astype(BF16)
                    wg_s[...] = wg
                    wu_s[...] = wu
                    wd_s[...] = wd
                else:
                    wg, wu, wd = wg_s[...], wu_s[...], wd_s[...]
                step = min(MOE_CHAIN_ROWS, n_rows)
                chains = [slice(r, r + step) for r in range(row0, row0 + n_rows, step)]
                xb = [_unpack_bf16_pairs(x_buf[slot, rows, :]) for rows in chains]
                hg = [jnp.dot(x, wg, preferred_element_type=F32) + bg_ref[0] for x in xb]
                hu = [jnp.dot(x, wu, preferred_element_type=F32) + bu_ref[0] for x in xb]
                act = []
                for g, u in zip(hg, hu):
                    g = jnp.minimum(g, SWIGLU_LIMIT)
                    u = jnp.clip(u, -SWIGLU_LIMIT, SWIGLU_LIMIT)
                    act.append(((u + 1.0) * g * jax.nn.sigmoid(SWIGLU_ALPHA * g)).astype(BF16))
                g_next = jnp.minimum(ig + 1, n_groups - 1)
                share = MOE_GROUP_ROWS // (D_FF // MOE_FF_TILE) // len(chains)
                for ic, (rows, a) in enumerate(zip(chains, act)):
                    o_ref[rows, :] += jnp.dot(a, wd, preferred_element_type=F32)
                    if row0 == 0:
                        for u in range(ic * share, (ic + 1) * share):
                            start_row(g_next, 1 - slot, jf * rows_per_step + u, u % 2)


def _moe_experts(h_packed, group_expert, group_valid, group_offset, sorted_tok, w_gu, b_gu, w_dn, b_dn):
    n_groups = group_expert.shape[0]
    d = 2 * h_packed.shape[1]
    nf = D_FF // MOE_FF_TILE
    gm, tf = MOE_GROUP_ROWS, MOE_FF_TILE

    def f_eff(ig, jf, gn):
        return jnp.where(gn[ig] > 0, jf, nf - 1)

    grid_spec = pltpu.PrefetchScalarGridSpec(
        num_scalar_prefetch=4,
        grid=(n_groups, nf),
        in_specs=[
            pl.BlockSpec(memory_space=pl.ANY),
            pl.BlockSpec((1, d, tf), lambda ig, jf, ge, gn, go, tk: (ge[ig], 0, f_eff(ig, jf, gn))),
            pl.BlockSpec((1, d, tf), lambda ig, jf, ge, gn, go, tk: (ge[ig], 0, nf + f_eff(ig, jf, gn))),
            pl.BlockSpec((1, 1, tf), lambda ig, jf, ge, gn, go, tk: (ge[ig], 0, f_eff(ig, jf, gn))),
            pl.BlockSpec((1, 1, tf), lambda ig, jf, ge, gn, go, tk: (ge[ig], 0, nf + f_eff(ig, jf, gn))),
            pl.BlockSpec((1, tf, d), lambda ig, jf, ge, gn, go, tk: (ge[ig], f_eff(ig, jf, gn), 0)),
            pl.BlockSpec((1, 1, d), lambda ig, jf, ge, gn, go, tk: (ge[ig], 0, 0)),
        ],
        out_specs=pl.BlockSpec((gm, d), lambda ig, jf, ge, gn, go, tk: (ig, 0)),
        scratch_shapes=[pltpu.VMEM((2, gm, d // 2), jnp.uint32), pltpu.SemaphoreType.DMA((2,)),
                        pltpu.VMEM((d, tf), BF16), pltpu.VMEM((d, tf), BF16), pltpu.VMEM((tf, d), BF16)],
    )
    vmem = 2 * gm * (d // 2) * 4 + 2 * (3 * d * tf * 4 + gm * d * 4) + 3 * d * tf * 2 + (6 << 20)
    return pl.pallas_call(
        _moe_kernel,
        grid_spec=grid_spec,
        out_shape=jax.ShapeDtypeStruct((n_groups * gm, d), F32),
        compiler_params=_params(2, vmem),
        name="moe_experts",
    )(group_expert, group_valid, group_offset, sorted_tok, h_packed, w_gu, w_gu, b_gu, b_gu, w_dn, b_dn)


def _combine_kernel(dest_ref, rows_hbm, gate_ref, h_ref, g_ref, b_ref, o_ref, buf, sem, *, tm):
    i = pl.program_id(0)
    n_steps = pl.num_programs(0)

    tokens_per_batch = DMA_ISSUE_UNROLL // TOP_K

    def start_rows(step, buf_slot):
        def body(ib, carry):
            for u in range(DMA_ISSUE_UNROLL):
                t = ib * tokens_per_batch + u // TOP_K
                kk = u % TOP_K
                src = dest_ref[(step * tm + t) * TOP_K + kk]
                pltpu.make_async_copy(rows_hbm.at[pl.ds(src, 1), :], buf.at[buf_slot, pl.ds(kk * tm + t, 1), :],
                                      sem.at[buf_slot]).start(priority=u % 2)
            return carry
        lax.fori_loop(0, tm // tokens_per_batch, body, 0)

    @pl.when(i == 0)
    def _():
        start_rows(0, 0)

    @pl.when(i + 1 < n_steps)
    def _():
        start_rows(i + 1, (i + 1) % 2)

    slot = i % 2
    pltpu.make_async_copy(rows_hbm.at[pl.ds(0, TOP_K * tm), :], buf.at[slot], sem.at[slot]).wait()
    y = sum(gate_ref[:, kk:kk + 1] * buf[slot, kk * tm:(kk + 1) * tm, :] for kk in range(TOP_K))
    o_ref[...] = _layer_norm(DEEPNORM_ALPHA * h_ref[...] + y, g_ref[...], b_ref[...])


def _combine(out_rows, dest, gate, h, g, b, *, tm):
    n, d = h.shape
    assert n % tm == 0
    grid_spec = pltpu.PrefetchScalarGridSpec(
        num_scalar_prefetch=1,
        grid=(n // tm,),
        in_specs=[pl.BlockSpec(memory_space=pl.ANY),
                  pl.BlockSpec((tm, TOP_K), lambda i, dest: (i, 0)),
                  pl.BlockSpec((tm, d), lambda i, dest: (i, 0)),
                  pl.BlockSpec((1, d), lambda i, dest: (0, 0)),
                  pl.BlockSpec((1, d), lambda i, dest: (0, 0))],
        out_specs=pl.BlockSpec((tm, d), lambda i, dest: (i, 0)),
        scratch_shapes=[pltpu.VMEM((2, TOP_K * tm, d), F32), pltpu.SemaphoreType.DMA((2,))],
    )
    vmem = 2 * TOP_K * tm * d * 4 + 2 * (2 * tm * d * 4 + tm * 128 * 4) + (4 << 20)
    return pl.pallas_call(
        functools.partial(_combine_kernel, tm=tm),
        grid_spec=grid_spec,
        out_shape=jax.ShapeDtypeStruct((n, d), F32),
        compiler_params=_params(1, vmem),
        name="moe_combine_ln",
    )(dest, out_rows, gate, h, g, b)


def _moe_ffn_ln(h, h_packed, logits, w_gu, b_gu, w_dn, b_dn, ln_g, ln_b):
    n, d = h.shape
    nk = n * TOP_K
    gm = MOE_GROUP_ROWS
    n_groups = N_EXPERTS + nk // gm
    top_logit, top_e = lax.top_k(logits, TOP_K)
    gate = jax.nn.softmax(top_logit, axis=-1)
    flat_e = top_e.reshape(nk)
    onehot = (flat_e[:, None] == jnp.arange(N_EXPERTS, dtype=flat_e.dtype)[None, :]).astype(jnp.int32)
    rank = jnp.take_along_axis(jnp.cumsum(onehot, axis=0), flat_e[:, None], axis=1)[:, 0] - 1
    counts = jnp.sum(onehot, axis=0)
    start_e = jnp.cumsum(counts) - counts
    groups_e = (counts + gm - 1) // gm
    gend_e = jnp.cumsum(groups_e)
    gstart_e = gend_e - groups_e
    dest = (gstart_e[flat_e] * gm + rank).astype(jnp.int32)
    sorted_tok = (jnp.argsort(flat_e, stable=True) // TOP_K).astype(jnp.int32)
    n_used = gend_e[-1]
    gidx = jnp.arange(n_groups, dtype=jnp.int32)
    gclamp = jnp.minimum(gidx, n_used - 1)
    group_expert = jnp.minimum(jnp.searchsorted(gend_e, gclamp, side="right"), N_EXPERTS - 1).astype(jnp.int32)
    within = gclamp - gstart_e[group_expert]
    used = gidx < n_used
    group_valid = jnp.where(used, jnp.clip(counts[group_expert] - within * gm, 0, gm), 0).astype(jnp.int32)
    group_offset = jnp.where(used, start_e[group_expert] + within * gm, 0).astype(jnp.int32)
    out_rows = _moe_experts(h_packed, group_expert, group_valid, group_offset, sorted_tok,
                            w_gu, b_gu.reshape(N_EXPERTS, 1, -1), w_dn, b_dn.reshape(N_EXPERTS, 1, -1))
    return _combine(out_rows, dest, gate, h, ln_g, ln_b, tm=COMBINE_ROWS)


def _pad_state(state):
    return jnp.pad(state, ((0, 0), (HALO - (CONV_W - 1), 0), (0, 0)))


def _last_rows(p, width, row0, bsz, t):
    if bsz == 1:
        return p[row0 + t - (CONV_W - 1):row0 + t, :width][None]
    return p[row0:row0 + bsz * t, :width].reshape(bsz, t, width)[:, t - (CONV_W - 1):]


def _dn_gates(p_ba, row0, bsz, t, chunk, a_log, dt_bias):
    ba = p_ba[row0:row0 + bsz * t, :2 * DN_V_HEADS].reshape(bsz, t, 2 * DN_V_HEADS)
    beta = jax.nn.sigmoid(ba[..., :DN_V_HEADS])
    g = -jnp.exp(a_log) * jax.nn.softplus(ba[..., DN_V_HEADS:] + dt_bias)
    n_ch = -(-t // chunk)

    def to5(u):
        u = jnp.pad(u, ((0, 0), (0, n_ch * chunk - t), (0, 0)))
        u = jnp.transpose(u, (0, 2, 1)).reshape(bsz, DN_QK_HEADS, 2, n_ch, chunk)
        return jnp.concatenate([u, u], axis=-1)

    return to5(g), to5(beta)


def kernel(x_prompt, x_sample, state_dn_conv, state_dn_ssm, state_lru_conv, state_lru_h, w_in, w_dn_conv, dn_a_log, dn_dt_bias, dn_norm_w, w_lru_conv, b_lru_conv, w_lru_ga, b_lru_ga, w_lru_gx, b_lru_gx, lru_lambda, w_proj_a, w_proj_b, w_out, ln1_g, ln1_b, w_router, b_router, w_gate_up, b_gate_up, w_down, b_down, ln2_g, ln2_b):
    assert w_in.shape[0] == 1, "single layer"
    bp, tp, d = x_prompt.shape
    bs, ts, _ = x_sample.shape
    np_, ns = bp * tp, bs * ts
    n = np_ + ns

    x_all = jnp.concatenate([x_prompt.reshape(np_, d), x_sample.reshape(ns, d)], axis=0)
    x_bf = x_all.astype(BF16)
    w_in_t = jnp.swapaxes(w_in[0], 0, 1)
    tm = n // 5
    p_a = _matmul_nt(x_bf, w_in_t, tm=tm, tn=1024, n_cols=OFF_B)
    p_ba = _matmul_nt(x_bf, w_in_t, tm=tm, tn=LANES, col0=OFF_B, n_cols=LANES)
    p_r = _matmul_nt(x_bf, w_in_t, tm=tm, tn=1024, col0=OFF_LX)

    zeros = lambda *shape: jnp.zeros(shape, F32)
    norm_w = dn_norm_w[0].reshape(1, DN_HEAD)
    g5, beta5 = _dn_gates(p_ba, 0, bp, tp, GDN_CHUNK, dn_a_log[0], dn_dt_bias[0])
    o_a, dn_s_p = _gated_delta(
        p_a, 0, bp, tp, zeros(bp, HALO, DN_QKV_DIM), w_dn_conv[0], g5, beta5,
        zeros(bp, DN_V_HEADS, DN_HEAD, DN_HEAD), norm_w, jnp.zeros((n, DN_V_DIM), BF16),
        chunk=GDN_CHUNK, n_chunks=GDN_CHUNKS_PER_STEP, n_qk=GDN_QK_HEADS_PER_STEP)
    g5, beta5 = _dn_gates(p_ba, np_, bs, ts, GDN_CHUNK, dn_a_log[0], dn_dt_bias[0])
    o_a, dn_s_s = _gated_delta(
        p_a, np_, bs, ts, _pad_state(state_dn_conv[0]), w_dn_conv[0], g5, beta5, state_dn_ssm[0], norm_w, o_a,
        chunk=GDN_CHUNK, n_chunks=1, n_qk=4)

    lru_args = (w_lru_conv[0], b_lru_conv, w_lru_ga[0], b_lru_ga[0].reshape(LRU_HEADS, 1, LRU_BW),
                w_lru_gx[0], b_lru_gx[0].reshape(LRU_HEADS, 1, LRU_BW),
                jax.nn.softplus(-lru_lambda[0]).reshape(1, LRU_W))
    o_b, h_p = _rglru(p_r, 0, bp, tp, zeros(bp, HALO, LRU_W), *lru_args, zeros(bp, 1, LRU_W),
                      jnp.zeros((n, LRU_W), BF16), rows=256)
    o_b, h_s = _rglru(p_r, np_, bs, ts, _pad_state(state_lru_conv[0]), *lru_args,
                      state_lru_h[0].reshape(bs, 1, LRU_W), o_b, rows=ts)

    merged = _merge(o_a, o_b, w_proj_a[0], w_proj_b[0], p_r, 2 * LRU_W, 3 * LRU_W, tm=n // 10, tn=256)
    h, h_packed, logits = _out_proj(merged, x_all, w_out[0], ln1_g, ln1_b, w_router[0], b_router, tm=n // 26)
    out = _moe_ffn_ln(h, h_packed, logits, w_gate_up[0], b_gate_up[0], w_down[0], b_down[0], ln2_g, ln2_b)

    return (out[:np_].reshape(bp, tp, d), out[np_:].reshape(bs, ts, d),
            _last_rows(p_a, DN_QKV_DIM, 0, bp, tp)[None], dn_s_p[None],
            _last_rows(p_r, LRU_W, 0, bp, tp)[None], h_p.reshape(bp, LRU_W)[None],
            _last_rows(p_a, DN_QKV_DIM, np_, bs, ts)[None], dn_s_s[None],
            _last_rows(p_r, LRU_W, np_, bs, ts)[None], h_s.reshape(bs, LRU_W)[None])
```

```python
import functools

import jax
import jax.numpy as jnp
from jax import lax
from jax.experimental import pallas as pl
from jax.experimental.pallas import tpu as pltpu

F32 = jnp.float32
BF16 = jnp.bfloat16

D_MODEL = 2048
DN_QK_HEADS = 16
DN_V_HEADS = 32
DN_HEAD = 128
DN_QK_DIM = DN_QK_HEADS * DN_HEAD
DN_V_DIM = DN_V_HEADS * DN_HEAD
DN_QKV_DIM = 2 * DN_QK_DIM + DN_V_DIM
CONV_W = 4
LRU_W = D_MODEL
LRU_HEADS = 16
LRU_BW = LRU_W // LRU_HEADS
LRU_C = 8.0
N_EXPERTS = 32
TOP_K = 4
D_FF = D_MODEL
SWIGLU_LIMIT = 7.0
SWIGLU_ALPHA = 1.702
DEEPNORM_ALPHA = 2.0 ** 0.25
LN_EPS = 1e-5
NORM_EPS = 1e-6
OFF_Z = DN_QKV_DIM
OFF_B = OFF_Z + DN_V_DIM
OFF_A = OFF_B + DN_V_HEADS
OFF_LX = OFF_A + DN_V_HEADS
OFF_LY = OFF_LX + LRU_W
OFF_GA = OFF_LY + LRU_W
OFF_GB = OFF_GA + D_MODEL

V7X_VMEM_LIMIT_BYTES = 56 * 1024 * 1024
SUBLANES = 8
LANES = 128

HALO = SUBLANES
GDN_CHUNK = 64
GDN_CHUNKS_PER_STEP = 4
GDN_QK_HEADS_PER_STEP = 4
GDN_WAVE_CHAINS = 32
MOE_GROUP_ROWS = 1280
MOE_SUB_BLOCKS = ((512, MOE_GROUP_ROWS, 0, 1024), (0, 512, 0, 512),
                  (1024, MOE_GROUP_ROWS, 1024, 128), (1152, MOE_GROUP_ROWS, 1152, 128))
MOE_CHAIN_ROWS = 256
MOE_FF_TILE = 256
COMBINE_ROWS = 128
DMA_ISSUE_UNROLL = 8


def _params(n_axes, vmem_bytes):
    limit = min(V7X_VMEM_LIMIT_BYTES, int(vmem_bytes * 1.2) + (4 << 20))
    return pltpu.CompilerParams(dimension_semantics=("arbitrary",) * n_axes, vmem_limit_bytes=limit)


def _layer_norm(x, g, b):
    mu = jnp.mean(x, axis=-1, keepdims=True)
    xc = x - mu
    var = jnp.mean(xc * xc, axis=-1, keepdims=True)
    return xc * lax.rsqrt(var + LN_EPS) * g + b


def _causal_conv_rows(tail, x, w):
    r = x.shape[0]
    xx = jnp.concatenate([tail, x], axis=0)
    first = HALO - (CONV_W - 1)
    return sum(xx[first + j:first + j + r] * w[j:j + 1] for j in range(CONV_W))


def _dot_nt(a, b):
    return lax.dot_general(a, b, (((1,), (1,)), ((), ())), preferred_element_type=F32)


def _mm_kernel(x_ref, wt_ref, o_ref):
    o_ref[...] = _dot_nt(x_ref[...], wt_ref[...].astype(BF16)).astype(o_ref.dtype)


def _mm_shift_kernel(x_ref, wa_ref, wb_ref, o_ref, *, shift):
    wt = jnp.concatenate([wa_ref[shift:, :], wb_ref[:shift, :]], axis=0)
    o_ref[...] = _dot_nt(x_ref[...], wt.astype(BF16)).astype(o_ref.dtype)


def _matmul_nt(x_bf, w_t, *, tm, tn, col0=0, n_cols=None):
    m, k = x_bf.shape
    n = w_t.shape[0] - col0 if n_cols is None else n_cols
    shift = col0 % tn
    assert m % tm == 0 and n % tn == 0 and shift % SUBLANES == 0 and shift <= LANES
    vmem = 2 * (tm * k * 2 + k * (tn + LANES) * 4 + tm * tn * 4) + k * tn * 6
    x_spec = pl.BlockSpec((tm, k), lambda i, j: (i, 0))
    w_spec = pl.BlockSpec((tn, k), lambda i, j: (col0 // tn + j, 0))
    if shift == 0:
        body, w_specs, w_args = _mm_kernel, [w_spec], [w_t]
    else:
        body = functools.partial(_mm_shift_kernel, shift=shift)
        w_specs = [w_spec, pl.BlockSpec((LANES, k), lambda i, j: ((col0 // tn + j + 1) * (tn // LANES), 0))]
        w_args = [w_t, w_t]
    return pl.pallas_call(
        body,
        grid=(m // tm, n // tn),
        in_specs=[x_spec] + w_specs,
        out_specs=pl.BlockSpec((tm, tn), lambda i, j: (i, j)),
        out_shape=jax.ShapeDtypeStruct((m, n), F32),
        compiler_params=_params(2, vmem),
        name="dense_matmul",
    )(x_bf, *w_args)


def _l2norm(u):
    return u * lax.rsqrt(jnp.sum(u * u, axis=-1, keepdims=True) + NORM_EPS)


def _gdn_kernel(q_ref, k_ref, v_ref, z_ref, qh_ref, kh_ref, vh_ref, qs_ref, ks_ref, vs_ref, wq_ref, wk_ref, wv_ref,
                g_ref, beta_ref, s0_ref, nw_ref, *rest, chunk, n_chunks, n_qk):
    o_ref, s_ref, hist_ref = rest[-3:]
    c = chunk

    @pl.when(pl.program_id(2) == 0)
    def _():
        s_ref[...] = s0_ref[...]

    cw = 2 * c
    ri = lax.broadcasted_iota(jnp.int32, (c, cw), 0)
    ci = lax.broadcasted_iota(jnp.int32, (c, cw), 1)
    as_f = lambda m: jnp.where(m, 1.0, 0.0).astype(F32)
    causal_f, strict_f, eye_f = as_f(ri >= ci), as_f(ri > ci), as_f(ri == ci)
    hi_f, eye_hi = as_f(ci >= c), as_f(ci == ri + c)
    half_lane = jnp.where(ci[:SUBLANES] >= c, ci[:SUBLANES] - c, ci[:SUBLANES])
    n_square = c.bit_length() - 1
    nw = nw_ref[...]
    valid = min(c, q_ref.shape[0])
    zeros_c = lambda width: jnp.zeros((c, width), F32)

    def col_of(row):
        return jnp.sum(row * eye_f, axis=1, keepdims=True)

    def conv_silu(x_ref, halo_ref, state_ref, w_ref, ic, lanes, hist_col0):
        lo = ic * c
        if ic == 0:
            cols = slice(hist_col0 + lanes.start, hist_col0 + lanes.stop)
            hist_ref[0:HALO, cols] = jnp.where(pl.program_id(2) == 0, state_ref[0, :, lanes], halo_ref[:, lanes])
            hist_ref[HALO:HALO + valid, cols] = x_ref[0:valid, lanes]
            src, first, src_lanes = hist_ref, HALO - (CONV_W - 1), cols
        else:
            src, first, src_lanes = x_ref, lo - (CONV_W - 1), lanes
        y = sum(src[first + j:first + j + valid, src_lanes] * w_ref[j:j + 1, lanes] for j in range(CONV_W))
        y = y / (1.0 + jnp.exp(-y))
        return y if valid == c else jnp.concatenate([y, jnp.zeros((c - valid, y.shape[1]), F32)], axis=0)

    heads = range(2 * n_qk)
    wq = n_qk * DN_HEAD
    pre = {}
    wave_chunks = max(1, GDN_WAVE_CHAINS // (2 * n_qk))
    for w0 in range(0, n_chunks, wave_chunks):
        ics = range(w0, min(w0 + wave_chunks, n_chunks))
        qkeys = [(ic, j) for ic in ics for j in range(n_qk)]
        keys = [(ic, hv) for ic in ics for hv in heads]
        q, k = {}, {}
        for ic, j in qkeys:
            qk_lanes = slice(j * DN_HEAD, (j + 1) * DN_HEAD)
            q[ic, j] = _l2norm(conv_silu(q_ref, qh_ref, qs_ref, wq_ref, ic, qk_lanes, 0)) * (DN_HEAD ** -0.5)
            k[ic, j] = _l2norm(conv_silu(k_ref, kh_ref, ks_ref, wk_ref, ic, qk_lanes, wq))
        kq, k_t = {}, {}
        for key in qkeys:
            k_t[key] = jnp.concatenate([k[key], zeros_c(DN_HEAD)], axis=0).T
            kq[key] = jnp.dot(jnp.concatenate([k[key], q[key]], axis=0).astype(BF16), k_t[key].astype(BF16),
                              preferred_element_type=F32)
        st, pt = {}, {}
        chunk_row = lambda ic: pl.ds(pl.program_id(2) * n_chunks + ic, 1)
        g_rows = jnp.concatenate([g_ref[0, hv // 2, hv % 2, chunk_row(ic), :] for ic, hv in keys], axis=0)
        gc_rows = g_rows
        for sh in [1 << b for b in range(c.bit_length() - 1)]:
            gc_rows = gc_rows + pltpu.roll(gc_rows, sh, axis=1) * as_f(half_lane[:1] >= sh)
        egc_rows = jnp.exp(gc_rows)
        for ik, (ic, hv) in enumerate(keys):
            j, hh = divmod(hv, 2)
            g_row = g_rows[ik:ik + 1]
            beta_row = beta_ref[0, j, hh, chunk_row(ic), :]
            beta_col = col_of(beta_row)
            gc_col = jnp.sum(g_row * causal_f, axis=1, keepdims=True)
            gc_row = gc_rows[ik:ik + 1]
            egc_row = egc_rows[ik:ik + 1]
            g_last = gc_row[:, c - 1:c]
            decay = jnp.exp(jnp.minimum(gc_col - gc_row, 0.0)) * causal_f
            st[ic, hv] = dict(
                t_scale=hi_f * beta_row, egc_row=egc_row, s_decay=jnp.exp(g_last),
                attn=(kq[ic, j][c:] * decay + eye_hi * egc_row).astype(BF16),
                k_dec_t=(k_t[ic, j] * jnp.exp(g_last - gc_row)).astype(BF16))
            pt[ic, hv] = eye_hi - (beta_col * kq[ic, j][:c] * decay) * strict_f
        for _ in range(n_square):
            for key in keys:
                cur = pt[key]
                pt[key] = jnp.dot(cur.astype(BF16), jnp.concatenate([cur, zeros_c(cw)], axis=0).astype(BF16),
                                  preferred_element_type=F32) + cur * hi_f
        u_base, k_cum = {}, {}
        for ic, hv in keys:
            a = st[ic, hv]
            v = conv_silu(v_ref, vh_ref, vs_ref, wv_ref, ic, slice(hv * DN_HEAD, (hv + 1) * DN_HEAD), 2 * wq)
            t_b = pt[ic, hv] * a["t_scale"]
            below = lambda m: jnp.concatenate([zeros_c(DN_HEAD), m], axis=0).astype(BF16)
            u_base[ic, hv] = jnp.dot(t_b.astype(BF16), below(v), preferred_element_type=F32)
            k_cum[ic, hv] = jnp.dot((t_b * a["egc_row"]).astype(BF16), below(k[ic, hv // 2]),
                                    preferred_element_type=F32)
        for ic, hv in keys:
            a = st[ic, hv]
            pre[ic, hv] = dict(u_base=u_base[ic, hv], attn=a["attn"], k_dec_t=a["k_dec_t"], s_decay=a["s_decay"],
                               kc_q=jnp.concatenate([k_cum[ic, hv], q[ic, hv // 2]], axis=0).astype(BF16))

    s = {hv: s_ref[0, hv] for hv in heads}
    for ic in range(n_chunks):
        ks_qs = {hv: jnp.dot(pre[ic, hv]["kc_q"], s[hv].astype(BF16), preferred_element_type=F32) for hv in heads}
        u = {hv: pre[ic, hv]["u_base"] - ks_qs[hv][:c] for hv in heads}
        o = {hv: jnp.dot(pre[ic, hv]["attn"], jnp.concatenate([u[hv], ks_qs[hv][c:]], axis=0).astype(BF16),
                         preferred_element_type=F32) for hv in heads}
        for hv in heads:
            s[hv] = s[hv] * pre[ic, hv]["s_decay"] + jnp.dot(
                pre[ic, hv]["k_dec_t"], jnp.concatenate([u[hv], zeros_c(DN_HEAD)], axis=0).astype(BF16),
                preferred_element_type=F32)
        for hv in heads:
            lanes = slice(hv * DN_HEAD, (hv + 1) * DN_HEAD)
            on = o[hv] * lax.rsqrt(jnp.mean(o[hv] * o[hv], axis=-1, keepdims=True) + NORM_EPS) * nw
            z = z_ref[ic * c:ic * c + valid, lanes]
            o_ref[ic * c:ic * c + valid, lanes] = (on[:valid] * (z * jax.nn.sigmoid(z))).astype(o_ref.dtype)
    for hv in heads:
        s_ref[0, hv] = s[hv]


def _gated_delta(p_a, row0, bsz, t, conv_state, w_conv, g5, beta5, s0, norm_w, o_init, *, chunk, n_chunks, n_qk):
    rows = min(t, chunk * n_chunks)
    nblk = t // rows
    assert t % rows == 0 and row0 % rows == 0 and (rows == chunk * n_chunks or n_chunks == 1)
    assert rows % HALO == 0
    rb0 = row0 // rows
    wq, wv = DN_HEAD * n_qk, 2 * DN_HEAD * n_qk
    kq0, kv0, kz0 = DN_QK_DIM // wq, 2 * DN_QK_DIM // wv, OFF_Z // wv

    def rows_spec(width, col0):
        return pl.BlockSpec((rows, width), lambda ib, ih, it: (rb0 + ib * nblk + it, col0 + ih))

    def halo_spec(width, col0):
        return pl.BlockSpec((HALO, width), lambda ib, ih, it: (
            jnp.maximum((rb0 + ib * nblk + it) * (rows // HALO) - 1, 0), col0 + ih))

    def state_spec(width, col0):
        return pl.BlockSpec((1, HALO, width), lambda ib, ih, it: (ib, 0, col0 + ih))

    def w_spec(width, col0):
        return pl.BlockSpec((CONV_W, width), lambda ib, ih, it: (0, col0 + ih))

    g_spec = pl.BlockSpec((1, n_qk, 2) + g5.shape[3:], lambda ib, ih, it: (ib, ih, 0, 0, 0))
    s_spec = pl.BlockSpec((1, 2 * n_qk, DN_HEAD, DN_HEAD), lambda ib, ih, it: (ib, ih, 0, 0))
    in_specs = [rows_spec(wq, 0), rows_spec(wq, kq0), rows_spec(wv, kv0), rows_spec(wv, kz0),
                halo_spec(wq, 0), halo_spec(wq, kq0), halo_spec(wv, kv0),
                state_spec(wq, 0), state_spec(wq, kq0), state_spec(wv, kv0),
                w_spec(wq, 0), w_spec(wq, kq0), w_spec(wv, kv0),
                g_spec, g_spec, s_spec, pl.BlockSpec((1, DN_HEAD), lambda ib, ih, it: (0, 0))]
    args = [p_a, p_a, p_a, p_a, p_a, p_a, p_a, conv_state, conv_state, conv_state,
            w_conv, w_conv, w_conv, g5, beta5, s0, norm_w]
    aliases = {}
    if o_init is not None:
        in_specs.append(pl.BlockSpec(memory_space=pl.ANY))
        args.append(o_init)
        aliases = {len(args) - 1: 0}
    vmem = 2 * (2 * rows * wq * 4 + 2 * rows * wv * 4 + rows * wv * 2 + 4 * n_qk * DN_HEAD * DN_HEAD * 4) + (16 << 20)
    return pl.pallas_call(
        functools.partial(_gdn_kernel, chunk=chunk, n_chunks=n_chunks, n_qk=n_qk),
        grid=(bsz, DN_QK_HEADS // n_qk, nblk),
        in_specs=in_specs,
        out_specs=[rows_spec(wv, 0), s_spec],
        out_shape=[jax.ShapeDtypeStruct((p_a.shape[0], DN_V_DIM), BF16),
                   jax.ShapeDtypeStruct((bsz, DN_V_HEADS, DN_HEAD, DN_HEAD), F32)],
        scratch_shapes=[pltpu.VMEM((HALO + min(chunk, rows), 2 * wq + wv), F32)],
        input_output_aliases=aliases,
        compiler_params=_params(3, vmem),
        name="gated_delta_rule",
    )(*args)


def _lru_kernel(x_ref, y_ref, halo_ref, state_ref, wc_ref, bc_ref, wga_ref, bga_ref, wgx_ref, bgx_ref, sp_ref, h0_ref,
                *rest, rows):
    o_ref, hl_ref, a_s, b_s, h_s = rest[-5:]

    @pl.when(pl.program_id(1) == 0)
    def _():
        h_s[...] = h0_ref[0]

    for hd in range(LRU_HEADS):
        lanes = slice(hd * LRU_BW, (hd + 1) * LRU_BW)
        tail = jnp.where(pl.program_id(1) == 0, state_ref[0, :, lanes], halo_ref[:, lanes])
        xh = _causal_conv_rows(tail, x_ref[:, lanes], wc_ref[:, lanes]) + bc_ref[:, lanes]
        xh_bf = xh.astype(BF16)
        r = jax.nn.sigmoid(jnp.dot(xh_bf, wga_ref[hd].astype(BF16), preferred_element_type=F32) + bga_ref[hd])
        i = jax.nn.sigmoid(jnp.dot(xh_bf, wgx_ref[hd].astype(BF16), preferred_element_type=F32) + bgx_ref[hd])
        log_a = (-LRU_C) * r * sp_ref[:, lanes]
        a = jnp.exp(log_a)
        a_s[:, lanes] = a
        b_s[:, lanes] = jnp.sqrt(1.0 - a * a) * (i * xh)

    row = lax.broadcasted_iota(jnp.int32, (SUBLANES, LRU_W), 0)

    def group(ig, h):
        r0 = pl.multiple_of(ig * SUBLANES, SUBLANES)
        a = a_s[pl.ds(r0, SUBLANES), :]
        b = b_s[pl.ds(r0, SUBLANES), :]
        for sh in (1, 2, 4):
            keep = row >= sh
            a_sh = jnp.where(keep, pltpu.roll(a, sh, axis=0), 1.0)
            b_sh = jnp.where(keep, pltpu.roll(b, sh, axis=0), 0.0)
            b = a * b_sh + b
            a = a * a_sh
        hs = a * h + b
        gy = jax.nn.gelu(y_ref[pl.ds(r0, SUBLANES), :])
        o_ref[pl.ds(r0, SUBLANES), :] = (hs * gy).astype(o_ref.dtype)
        return hs[SUBLANES - 1:SUBLANES, :]

    h = lax.fori_loop(0, rows // SUBLANES, group, h_s[...])
    h_s[...] = h
    hl_ref[0] = h


def _rglru(p_r, row0, bsz, t, conv_state, w_conv, b_conv, w_ga, b_ga, w_gx, b_gx, sp, h0, o_init, *, rows):
    w = LRU_W
    nblk = t // rows
    assert t % rows == 0 and rows % SUBLANES == 0 and row0 % rows == 0
    rb0 = row0 // rows
    rows_spec = lambda col: pl.BlockSpec((rows, w), lambda ib, it: (rb0 + ib * nblk + it, col))
    full3 = lambda shape: pl.BlockSpec(shape, lambda ib, it: (0, 0, 0))
    full2 = lambda shape: pl.BlockSpec(shape, lambda ib, it: (0, 0))
    h_spec = pl.BlockSpec((1, 1, w), lambda ib, it: (ib, 0, 0))
    in_specs = [rows_spec(0), rows_spec(1),
                pl.BlockSpec((HALO, w), lambda ib, it: (
                    jnp.maximum((rb0 + ib * nblk + it) * (rows // HALO) - 1, 0), 0)),
                pl.BlockSpec((1, HALO, w), lambda ib, it: (ib, 0, 0)),
                full2((CONV_W, w)), full2((1, w)),
                full3((LRU_HEADS, LRU_BW, LRU_BW)), full3((LRU_HEADS, 1, LRU_BW)),
                full3((LRU_HEADS, LRU_BW, LRU_BW)), full3((LRU_HEADS, 1, LRU_BW)),
                full2((1, w)), h_spec]
    args = [p_r, p_r, p_r, conv_state, w_conv, b_conv, w_ga, b_ga, w_gx, b_gx, sp, h0]
    aliases = {}
    if o_init is not None:
        in_specs.append(pl.BlockSpec(memory_space=pl.ANY))
        args.append(o_init)
        aliases = {len(args) - 1: 0}
    vmem = 2 * (2 * rows * w * 4 + rows * w * 2 + 4 * LRU_HEADS * LRU_BW * LRU_BW * 4) + 2 * rows * w * 4 + (8 << 20)
    return pl.pallas_call(
        functools.partial(_lru_kernel, rows=rows),
        grid=(bsz, nblk),
        in_specs=in_specs,
        out_specs=[rows_spec(0), h_spec],
        out_shape=[jax.ShapeDtypeStruct((p_r.shape[0], w), BF16), jax.ShapeDtypeStruct((bsz, 1, w), F32)],
        scratch_shapes=[pltpu.VMEM((rows, w), F32), pltpu.VMEM((rows, w), F32), pltpu.VMEM((1, w), F32)],
        input_output_aliases=aliases,
        compiler_params=_params(2, vmem),
        name="rg_lru",
    )(*args)


def _merge_kernel(oa_ref, ob_ref, wa_ref, wb_ref, ga_ref, gb_ref, o_ref):
    ya = jnp.dot(oa_ref[...], wa_ref[...].astype(BF16), preferred_element_type=F32)
    yb = jnp.dot(ob_ref[...], wb_ref[...].astype(BF16), preferred_element_type=F32)
    o_ref[...] = (jax.nn.sigmoid(ga_ref[...]) * ya + jax.nn.sigmoid(gb_ref[...]) * yb).astype(o_ref.dtype)


def _merge(oa, ob, wa, wb, gates, ga_col0, gb_col0, *, tm, tn):
    m = oa.shape[0]
    n = wa.shape[1]
    assert m % tm == 0 and n % tn == 0 and ga_col0 % tn == 0 and gb_col0 % tn == 0
    ka, kb = oa.shape[1], ob.shape[1]
    vmem = 2 * (tm * (ka + kb) * 2 + (ka + kb) * tn * 4 + 2 * tm * tn * 4 + tm * tn * 2) + (ka + kb) * tn * 2
    row = lambda kdim: pl.BlockSpec((tm, kdim), lambda i, j: (i, 0))
    col = lambda kdim: pl.BlockSpec((kdim, tn), lambda i, j: (0, j))
    gate = lambda col0: pl.BlockSpec((tm, tn), lambda i, j: (i, col0 // tn + j))
    return pl.pallas_call(
        _merge_kernel,
        grid=(m // tm, n // tn),
        in_specs=[row(ka), row(kb), col(ka), col(kb), gate(ga_col0), gate(gb_col0)],
        out_specs=pl.BlockSpec((tm, tn), lambda i, j: (i, j)),
        out_shape=jax.ShapeDtypeStruct((m, n), BF16),
        compiler_params=_params(2, vmem),
        name="gated_merge",
    )(oa, ob, wa, wb, gates, gates)


def _pack_bf16_pairs(x):
    half = x.shape[1] // 2
    bits = lambda u: pltpu.bitcast(u.astype(BF16).astype(F32), jnp.uint32)
    return (bits(x[:, :half]) >> 16) | (bits(x[:, half:]) & jnp.uint32(0xFFFF0000))


def _unpack_bf16_pairs(words):
    lo = pltpu.bitcast(words << 16, F32)
    hi = pltpu.bitcast(words & jnp.uint32(0xFFFF0000), F32)
    return jnp.concatenate([lo, hi], axis=1).astype(BF16)


def _out_kernel(m_ref, x_ref, w_ref, g_ref, b_ref, wr_ref, br_ref, h_ref, hp_ref, lg_ref):
    y = jnp.dot(m_ref[...], w_ref[...].astype(BF16), preferred_element_type=F32)
    h = _layer_norm(DEEPNORM_ALPHA * x_ref[...] + y, g_ref[...], b_ref[...])
    h_ref[...] = h
    hp_ref[...] = _pack_bf16_pairs(h)
    split = lambda u: (u.astype(BF16), (u - u.astype(BF16).astype(F32)).astype(BF16))
    h_hi, h_lo = split(h)
    w_hi, w_lo = split(wr_ref[...])
    rows = h.shape[0]
    both = jnp.dot(jnp.concatenate([h_hi, h_lo], axis=0), w_hi, preferred_element_type=F32)
    lg_ref[...] = both[:rows] + both[rows:] + jnp.dot(h_hi, w_lo, preferred_element_type=F32) + br_ref[...]


def _out_proj(merged, x, w_out, g, b, w_router, b_router, *, tm):
    m, d = x.shape
    e = w_router.shape[1]
    assert m % tm == 0
    vmem = 2 * (tm * d * 2 + tm * d * 4 + d * d * 4 + tm * d * 6 + d * e * 4 + tm * 128 * 4) + d * d * 2
    rows = lambda width: pl.BlockSpec((tm, width), lambda i: (i, 0))
    full = lambda shape: pl.BlockSpec(shape, lambda i: (0, 0))
    return pl.pallas_call(
        _out_kernel,
        grid=(m // tm,),
        in_specs=[rows(d), rows(d), full((d, d)), full((1, d)), full((1, d)), full((d, e)), full((1, e))],
        out_specs=[rows(d), rows(d // 2), rows(e)],
        out_shape=[jax.ShapeDtypeStruct((m, d), F32), jax.ShapeDtypeStruct((m, d // 2), jnp.uint32),
                   jax.ShapeDtypeStruct((m, e), F32)],
        compiler_params=_params(1, vmem),
        name="out_proj_ln_router",
    )(merged, x, w_out, g, b, w_router, b_router)


def _sub_block_runs(n_valid, lo, hi):
    return (n_valid > lo) & (n_valid <= hi)


def _moe_kernel(ge_ref, gn_ref, gofs_ref, tok_ref, h_hbm, wg_ref, wu_ref, bg_ref, bu_ref, wd_ref, bd_ref, o_ref,
                x_buf, sem, wg_s, wu_s, wd_s):
    ig = pl.program_id(0)
    jf = pl.program_id(1)
    n_valid = gn_ref[ig]
    slot = ig % 2

    n_groups = pl.num_programs(0)
    rows_per_step = MOE_GROUP_ROWS // pl.num_programs(1)

    def start_row(g, buf_slot, r, priority):
        tok = tok_ref[gofs_ref[g] + jnp.minimum(r, jnp.maximum(gn_ref[g] - 1, 0))]
        pltpu.make_async_copy(h_hbm.at[pl.ds(tok, 1), :], x_buf.at[buf_slot, pl.ds(r, 1), :],
                              sem.at[buf_slot]).start(priority=priority)

    def wait_rows(buf_slot):
        pltpu.make_async_copy(h_hbm.at[pl.ds(0, MOE_GROUP_ROWS), :], x_buf.at[buf_slot], sem.at[buf_slot]).wait()

    @pl.when(jf == 0)
    def _():
        @pl.when(ig == 0)
        def _():
            def body(ib, carry):
                for u in range(DMA_ISSUE_UNROLL):
                    start_row(0, 0, ib * DMA_ISSUE_UNROLL + u, u % 2)
                return carry
            lax.fori_loop(0, MOE_GROUP_ROWS // DMA_ISSUE_UNROLL, body, 0)

        @pl.when((ig == 0) | (gn_ref[jnp.maximum(ig - 1, 0)] > 0))
        def _():
            wait_rows(slot)

    @pl.when((ig == n_groups - 1) & (jf == pl.num_programs(1) - 1) & (n_valid > 0))
    def _():
        wait_rows(1 - slot)

    @pl.when((n_valid == 0) & (jf == 0))
    def _():
        o_ref[...] = jnp.zeros(o_ref.shape, o_ref.dtype)

    @pl.when(n_valid > 0)
    def _():
        @pl.when(jf == 0)
        def _():
            o_ref[...] = jnp.broadcast_to(bd_ref[0], o_ref.shape)

        for lo, hi, row0, n_rows in MOE_SUB_BLOCKS:
            @pl.when(_sub_block_runs(n_valid, lo, hi))
            def _():
                if row0 == 0:
                    wg = wg_ref[0].astype(BF16)
                    wu = wu_ref[0].astype(BF16)
                    wd = wd_ref[0].astype(BF16)
                    wg_s[...] = wg
                    wu_s[...] = wu
                    wd_s[...] = wd
                else:
                    wg, wu, wd = wg_s[...], wu_s[...], wd_s[...]
                step = min(MOE_CHAIN_ROWS, n_rows)
                chains = [slice(r, r + step) for r in range(row0, row0 + n_rows, step)]
                xb = [_unpack_bf16_pairs(x_buf[slot, rows, :]) for rows in chains]
                hg = [jnp.dot(x, wg, preferred_element_type=F32) + bg_ref[0] for x in xb]
                hu = [jnp.dot(x, wu, preferred_element_type=F32) + bu_ref[0] for x in xb]
                act = []
                for g, u in zip(hg, hu):
                    g = jnp.minimum(g, SWIGLU_LIMIT)
                    u = jnp.clip(u, -SWIGLU_LIMIT, SWIGLU_LIMIT)
                    act.append(((u + 1.0) * g * jax.nn.sigmoid(SWIGLU_ALPHA * g)).astype(BF16))
                g_next = jnp.minimum(ig + 1, n_groups - 1)
                share = MOE_GROUP_ROWS // (D_FF // MOE_FF_TILE) // len(chains)
                for ic, (rows, a) in enumerate(zip(chains, act)):
                    o_ref[rows, :] += jnp.dot(a, wd, preferred_element_type=F32)
                    if row0 == 0:
                        for u in range(ic * share, (ic + 1) * share):
                            start_row(g_next, 1 - slot, jf * rows_per_step + u, u % 2)


def _moe_experts(h_packed, group_expert, group_valid, group_offset, sorted_tok, w_gu, b_gu, w_dn, b_dn):
    n_groups = group_expert.shape[0]
    d = 2 * h_packed.shape[1]
    nf = D_FF // MOE_FF_TILE
    gm, tf = MOE_GROUP_ROWS, MOE_FF_TILE

    def f_eff(ig, jf, gn):
        return jnp.where(gn[ig] > 0, jf, nf - 1)

    grid_spec = pltpu.PrefetchScalarGridSpec(
        num_scalar_prefetch=4,
        grid=(n_groups, nf),
        in_specs=[
            pl.BlockSpec(memory_space=pl.ANY),
            pl.BlockSpec((1, d, tf), lambda ig, jf, ge, gn, go, tk: (ge[ig], 0, f_eff(ig, jf, gn))),
            pl.BlockSpec((1, d, tf), lambda ig, jf, ge, gn, go, tk: (ge[ig], 0, nf + f_eff(ig, jf, gn))),
            pl.BlockSpec((1, 1, tf), lambda ig, jf, ge, gn, go, tk: (ge[ig], 0, f_eff(ig, jf, gn))),
            pl.BlockSpec((1, 1, tf), lambda ig, jf, ge, gn, go, tk: (ge[ig], 0, nf + f_eff(ig, jf, gn))),
            pl.BlockSpec((1, tf, d), lambda ig, jf, ge, gn, go, tk: (ge[ig], f_eff(ig, jf, gn), 0)),
            pl.BlockSpec((1, 1, d), lambda ig, jf, ge, gn, go, tk: (ge[ig], 0, 0)),
        ],
        out_specs=pl.BlockSpec((gm, d), lambda ig, jf, ge, gn, go, tk: (ig, 0)),
        scratch_shapes=[pltpu.VMEM((2, gm, d // 2), jnp.uint32), pltpu.SemaphoreType.DMA((2,)),
                        pltpu.VMEM((d, tf), BF16), pltpu.VMEM((d, tf), BF16), pltpu.VMEM((tf, d), BF16)],
    )
    vmem = 2 * gm * (d // 2) * 4 + 2 * (3 * d * tf * 4 + gm * d * 4) + 3 * d * tf * 2 + (6 << 20)
    return pl.pallas_call(
        _moe_kernel,
        grid_spec=grid_spec,
        out_shape=jax.ShapeDtypeStruct((n_groups * gm, d), F32),
        compiler_params=_params(2, vmem),
        name="moe_experts",
    )(group_expert, group_valid, group_offset, sorted_tok, h_packed, w_gu, w_gu, b_gu, b_gu, w_dn, b_dn)


def _combine_kernel(dest_ref, rows_hbm, gate_ref, h_ref, g_ref, b_ref, oa_ref, ob_ref, buf, sem, *, tm, steps_a):
    i = pl.program_id(0)
    n_steps = pl.num_programs(0)

    tokens_per_batch = DMA_ISSUE_UNROLL // TOP_K

    def start_rows(step, buf_slot):
        def body(ib, carry):
            for u in range(DMA_ISSUE_UNROLL):
                t = ib * tokens_per_batch + u // TOP_K
                kk = u % TOP_K
                src = dest_ref[(step * tm + t) * TOP_K + kk]
                pltpu.make_async_copy(rows_hbm.at[pl.ds(src, 1), :], buf.at[buf_slot, pl.ds(kk * tm + t, 1), :],
                                      sem.at[buf_slot]).start(priority=u % 2)
            return carry
        lax.fori_loop(0, tm // tokens_per_batch, body, 0)

    @pl.when(i == 0)
    def _():
        start_rows(0, 0)

    @pl.when(i + 1 < n_steps)
    def _():
        start_rows(i + 1, (i + 1) % 2)

    slot = i % 2
    pltpu.make_async_copy(rows_hbm.at[pl.ds(0, TOP_K * tm), :], buf.at[slot], sem.at[slot]).wait()
    y = sum(gate_ref[:, kk:kk + 1] * buf[slot, kk * tm:(kk + 1) * tm, :] for kk in range(TOP_K))
    out = _layer_norm(DEEPNORM_ALPHA * h_ref[...] + y, g_ref[...], b_ref[...])

    @pl.when(i < steps_a)
    def _():
        oa_ref[...] = out

    @pl.when(i >= steps_a)
    def _():
        ob_ref[...] = out


def _combine(out_rows, dest, gate, h, g, b, *, tm, n_first):
    n, d = h.shape
    assert n % tm == 0 and n_first % tm == 0 and 0 < n_first < n
    steps_a = n_first // tm
    grid_spec = pltpu.PrefetchScalarGridSpec(
        num_scalar_prefetch=1,
        grid=(n // tm,),
        in_specs=[pl.BlockSpec(memory_space=pl.ANY),
                  pl.BlockSpec((tm, TOP_K), lambda i, dest: (i, 0)),
                  pl.BlockSpec((tm, d), lambda i, dest: (i, 0)),
                  pl.BlockSpec((1, d), lambda i, dest: (0, 0)),
                  pl.BlockSpec((1, d), lambda i, dest: (0, 0))],
        out_specs=[pl.BlockSpec((tm, d), lambda i, dest: (jnp.minimum(i, steps_a - 1), 0)),
                   pl.BlockSpec((tm, d), lambda i, dest: (jnp.maximum(i - steps_a, 0), 0))],
        scratch_shapes=[pltpu.VMEM((2, TOP_K * tm, d), F32), pltpu.SemaphoreType.DMA((2,))],
    )
    vmem = 2 * TOP_K * tm * d * 4 + 2 * (3 * tm * d * 4 + tm * 128 * 4) + (4 << 20)
    return pl.pallas_call(
        functools.partial(_combine_kernel, tm=tm, steps_a=steps_a),
        grid_spec=grid_spec,
        out_shape=[jax.ShapeDtypeStruct((n_first, d), F32), jax.ShapeDtypeStruct((n - n_first, d), F32)],
        compiler_params=_params(1, vmem),
        name="moe_combine_ln",
    )(dest, out_rows, gate, h, g, b)


def _moe_ffn_ln(h, h_packed, logits, w_gu, b_gu, w_dn, b_dn, ln_g, ln_b, n_first):
    n, d = h.shape
    nk = n * TOP_K
    gm = MOE_GROUP_ROWS
    n_groups = N_EXPERTS + nk // gm
    top_logit, top_e = lax.top_k(logits, TOP_K)
    gate = jax.nn.softmax(top_logit, axis=-1)
    flat_e = top_e.reshape(nk)
    onehot = (flat_e[:, None] == jnp.arange(N_EXPERTS, dtype=flat_e.dtype)[None, :]).astype(jnp.int32)
    rank = jnp.take_along_axis(jnp.cumsum(onehot, axis=0), flat_e[:, None], axis=1)[:, 0] - 1
    counts = jnp.sum(onehot, axis=0)
    start_e = jnp.cumsum(counts) - counts
    groups_e = (counts + gm - 1) // gm
    gend_e = jnp.cumsum(groups_e)
    gstart_e = gend_e - groups_e
    dest = (gstart_e[flat_e] * gm + rank).astype(jnp.int32)
    sorted_tok = (jnp.argsort(flat_e, stable=True) // TOP_K).astype(jnp.int32)
    n_used = gend_e[-1]
    gidx = jnp.arange(n_groups, dtype=jnp.int32)
    gclamp = jnp.minimum(gidx, n_used - 1)
    group_expert = jnp.minimum(jnp.searchsorted(gend_e, gclamp, side="right"), N_EXPERTS - 1).astype(jnp.int32)
    within = gclamp - gstart_e[group_expert]
    used = gidx < n_used
    group_valid = jnp.where(used, jnp.clip(counts[group_expert] - within * gm, 0, gm), 0).astype(jnp.int32)
    group_offset = jnp.where(used, start_e[group_expert] + within * gm, 0).astype(jnp.int32)
    out_rows = _moe_experts(h_packed, group_expert, group_valid, group_offset, sorted_tok,
                            w_gu, b_gu.reshape(N_EXPERTS, 1, -1), w_dn, b_dn.reshape(N_EXPERTS, 1, -1))
    return _combine(out_rows, dest, gate, h, ln_g, ln_b, tm=COMBINE_ROWS, n_first=n_first)


def _pad_state(state):
    return jnp.pad(state, ((0, 0), (HALO - (CONV_W - 1), 0), (0, 0)))


def _last_rows(p, width, row0, bsz, t):
    if bsz == 1:
        return p[row0 + t - (CONV_W - 1):row0 + t, :width][None]
    return p[row0:row0 + bsz * t, :width].reshape(bsz, t, width)[:, t - (CONV_W - 1):]


def _dn_gates(p_ba, row0, bsz, t, chunk, a_log, dt_bias):
    ba = p_ba[row0:row0 + bsz * t, :2 * DN_V_HEADS].reshape(bsz, t, 2 * DN_V_HEADS)
    beta = jax.nn.sigmoid(ba[..., :DN_V_HEADS])
    g = -jnp.exp(a_log) * jax.nn.softplus(ba[..., DN_V_HEADS:] + dt_bias)
    n_ch = -(-t // chunk)

    def to5(u):
        u = jnp.pad(u, ((0, 0), (0, n_ch * chunk - t), (0, 0)))
        u = jnp.transpose(u, (0, 2, 1)).reshape(bsz, DN_QK_HEADS, 2, n_ch, chunk)
        return jnp.concatenate([u, u], axis=-1)

    return to5(g), to5(beta)


def kernel(x_prompt, x_sample, state_dn_conv, state_dn_ssm, state_lru_conv, state_lru_h, w_in, w_dn_conv, dn_a_log, dn_dt_bias, dn_norm_w, w_lru_conv, b_lru_conv, w_lru_ga, b_lru_ga, w_lru_gx, b_lru_gx, lru_lambda, w_proj_a, w_proj_b, w_out, ln1_g, ln1_b, w_router, b_router, w_gate_up, b_gate_up, w_down, b_down, ln2_g, ln2_b):
    assert w_in.shape[0] == 1, "single layer"
    bp, tp, d = x_prompt.shape
    bs, ts, _ = x_sample.shape
    np_, ns = bp * tp, bs * ts
    n = np_ + ns

    x_all = jnp.concatenate([x_prompt.reshape(np_, d), x_sample.reshape(ns, d)], axis=0)
    x_bf = x_all.astype(BF16)
    w_in_t = jnp.swapaxes(w_in[0], 0, 1)
    tm = n // 5
    p_a = _matmul_nt(x_bf, w_in_t, tm=tm, tn=1024, n_cols=OFF_B)
    p_ba = _matmul_nt(x_bf, w_in_t, tm=tm, tn=LANES, col0=OFF_B, n_cols=LANES)
    p_r = _matmul_nt(x_bf, w_in_t, tm=tm, tn=1024, col0=OFF_LX)

    zeros = lambda *shape: jnp.zeros(shape, F32)
    norm_w = dn_norm_w[0].reshape(1, DN_HEAD)
    g5, beta5 = _dn_gates(p_ba, 0, bp, tp, GDN_CHUNK, dn_a_log[0], dn_dt_bias[0])
    o_a, dn_s_p = _gated_delta(
        p_a, 0, bp, tp, zeros(bp, HALO, DN_QKV_DIM), w_dn_conv[0], g5, beta5,
        zeros(bp, DN_V_HEADS, DN_HEAD, DN_HEAD), norm_w, jnp.zeros((n, DN_V_DIM), BF16),
        chunk=GDN_CHUNK, n_chunks=GDN_CHUNKS_PER_STEP, n_qk=GDN_QK_HEADS_PER_STEP)
    g5, beta5 = _dn_gates(p_ba, np_, bs, ts, GDN_CHUNK, dn_a_log[0], dn_dt_bias[0])
    o_a, dn_s_s = _gated_delta(
        p_a, np_, bs, ts, _pad_state(state_dn_conv[0]), w_dn_conv[0], g5, beta5, state_dn_ssm[0], norm_w, o_a,
        chunk=GDN_CHUNK, n_chunks=1, n_qk=4)

    lru_args = (w_lru_conv[0], b_lru_conv, w_lru_ga[0], b_lru_ga[0].reshape(LRU_HEADS, 1, LRU_BW),
                w_lru_gx[0], b_lru_gx[0].reshape(LRU_HEADS, 1, LRU_BW),
                jax.nn.softplus(-lru_lambda[0]).reshape(1, LRU_W))
    o_b, h_p = _rglru(p_r, 0, bp, tp, zeros(bp, HALO, LRU_W), *lru_args, zeros(bp, 1, LRU_W),
                      jnp.zeros((n, LRU_W), BF16), rows=256)
    o_b, h_s = _rglru(p_r, np_, bs, ts, _pad_state(state_lru_conv[0]), *lru_args,
                      state_lru_h[0].reshape(bs, 1, LRU_W), o_b, rows=ts)

    merged = _merge(o_a, o_b, w_proj_a[0], w_proj_b[0], p_r, 2 * LRU_W, 3 * LRU_W, tm=n // 10, tn=256)
    h, h_packed, logits = _out_proj(merged, x_all, w_out[0], ln1_g, ln1_b, w_router[0], b_router, tm=n // 26)
    y_p, y_s = _moe_ffn_ln(h, h_packed, logits, w_gate_up[0], b_gate_up[0], w_down[0], b_down[0], ln2_g, ln2_b, np_)

    return (y_p.reshape(bp, tp, d), y_s.reshape(bs, ts, d),
            _last_rows(p_a, DN_QKV_DIM, 0, bp, tp)[None], dn_s_p[None],
            _last_rows(p_r, LRU_W, 0, bp, tp)[None], h_p.reshape(bp, LRU_W)[None],
            _last_rows(p_a, DN_QKV_DIM, np_, bs, ts)[None], dn_s_s[None],
            _last_rows(p_r, LRU_W, np_, bs, ts)[None], h_s.reshape(bs, LRU_W)[None])
```

```python
import functools

import jax
import jax.numpy as jnp
from jax import lax
from jax.experimental import pallas as pl
from jax.experimental.pallas import tpu as pltpu

F32 = jnp.float32
BF16 = jnp.bfloat16

D_MODEL = 2048
DN_QK_HEADS = 16
DN_V_HEADS = 32
DN_HEAD = 128
DN_QK_DIM = DN_QK_HEADS * DN_HEAD
DN_V_DIM = DN_V_HEADS * DN_HEAD
DN_QKV_DIM = 2 * DN_QK_DIM + DN_V_DIM
CONV_W = 4
LRU_W = D_MODEL
LRU_HEADS = 16
LRU_BW = LRU_W // LRU_HEADS
LRU_C = 8.0
N_EXPERTS = 32
TOP_K = 4
D_FF = D_MODEL
SWIGLU_LIMIT = 7.0
SWIGLU_ALPHA = 1.702
DEEPNORM_ALPHA = 2.0 ** 0.25
LN_EPS = 1e-5
NORM_EPS = 1e-6
OFF_Z = DN_QKV_DIM
OFF_B = OFF_Z + DN_V_DIM
OFF_A = OFF_B + DN_V_HEADS
OFF_LX = OFF_A + DN_V_HEADS
OFF_LY = OFF_LX + LRU_W
OFF_GA = OFF_LY + LRU_W
OFF_GB = OFF_GA + D_MODEL

V7X_VMEM_LIMIT_BYTES = 56 * 1024 * 1024
SUBLANES = 8
LANES = 128

HALO = SUBLANES
GDN_CHUNK = 64
GDN_CHUNKS_PER_STEP = 4
GDN_QK_HEADS_PER_STEP = 4
GDN_WAVE_CHAINS = 32
MOE_GROUP_ROWS = 1280
MOE_SUB_BLOCKS = ((1152, MOE_GROUP_ROWS, 0, 1280), (1024, 1152, 0, 1152), (512, 1024, 0, 1024), (0, 512, 0, 512))
MOE_CHAIN_ROWS = 256
MOE_FF_TILE = 256
COMBINE_ROWS = 128
DMA_ISSUE_UNROLL = 8


def _params(n_axes, vmem_bytes):
    limit = min(V7X_VMEM_LIMIT_BYTES, int(vmem_bytes * 1.2) + (4 << 20))
    return pltpu.CompilerParams(dimension_semantics=("arbitrary",) * n_axes, vmem_limit_bytes=limit)


def _layer_norm(x, g, b):
    mu = jnp.mean(x, axis=-1, keepdims=True)
    xc = x - mu
    var = jnp.mean(xc * xc, axis=-1, keepdims=True)
    return xc * lax.rsqrt(var + LN_EPS) * g + b


def _causal_conv_rows(tail, x, w):
    r = x.shape[0]
    xx = jnp.concatenate([tail, x], axis=0)
    first = HALO - (CONV_W - 1)
    return sum(xx[first + j:first + j + r] * w[j:j + 1] for j in range(CONV_W))


def _dot_nt(a, b):
    return lax.dot_general(a, b, (((1,), (1,)), ((), ())), preferred_element_type=F32)


def _mm_kernel(x_ref, wt_ref, o_ref):
    o_ref[...] = _dot_nt(x_ref[...], wt_ref[...].astype(BF16)).astype(o_ref.dtype)


def _mm_shift_kernel(x_ref, wa_ref, wb_ref, o_ref, *, shift):
    wt = jnp.concatenate([wa_ref[shift:, :], wb_ref[:shift, :]], axis=0)
    o_ref[...] = _dot_nt(x_ref[...], wt.astype(BF16)).astype(o_ref.dtype)


def _matmul_nt(x_bf, w_t, *, tm, tn, col0=0, n_cols=None):
    m, k = x_bf.shape
    n = w_t.shape[0] - col0 if n_cols is None else n_cols
    shift = col0 % tn
    assert m % tm == 0 and n % tn == 0 and shift % SUBLANES == 0 and shift <= LANES
    vmem = 2 * (tm * k * 2 + k * (tn + LANES) * 4 + tm * tn * 4) + k * tn * 6
    x_spec = pl.BlockSpec((tm, k), lambda i, j: (i, 0))
    w_spec = pl.BlockSpec((tn, k), lambda i, j: (col0 // tn + j, 0))
    if shift == 0:
        body, w_specs, w_args = _mm_kernel, [w_spec], [w_t]
    else:
        body = functools.partial(_mm_shift_kernel, shift=shift)
        w_specs = [w_spec, pl.BlockSpec((LANES, k), lambda i, j: ((col0 // tn + j + 1) * (tn // LANES), 0))]
        w_args = [w_t, w_t]
    return pl.pallas_call(
        body,
        grid=(m // tm, n // tn),
        in_specs=[x_spec] + w_specs,
        out_specs=pl.BlockSpec((tm, tn), lambda i, j: (i, j)),
        out_shape=jax.ShapeDtypeStruct((m, n), F32),
        compiler_params=_params(2, vmem),
        name="dense_matmul",
    )(x_bf, *w_args)


def _l2norm(u):
    return u * lax.rsqrt(jnp.sum(u * u, axis=-1, keepdims=True) + NORM_EPS)


def _gdn_kernel(q_ref, k_ref, v_ref, z_ref, qh_ref, kh_ref, vh_ref, qs_ref, ks_ref, vs_ref, wq_ref, wk_ref, wv_ref,
                g_ref, beta_ref, s0_ref, nw_ref, *rest, chunk, n_chunks, n_qk):
    o_ref, s_ref, hist_ref = rest[-3:]
    c = chunk

    @pl.when(pl.program_id(2) == 0)
    def _():
        s_ref[...] = s0_ref[...]

    cw = 2 * c
    ri = lax.broadcasted_iota(jnp.int32, (c, cw), 0)
    ci = lax.broadcasted_iota(jnp.int32, (c, cw), 1)
    as_f = lambda m: jnp.where(m, 1.0, 0.0).astype(F32)
    causal_f, strict_f, eye_f = as_f(ri >= ci), as_f(ri > ci), as_f(ri == ci)
    hi_f, eye_hi = as_f(ci >= c), as_f(ci == ri + c)
    half_lane = jnp.where(ci[:SUBLANES] >= c, ci[:SUBLANES] - c, ci[:SUBLANES])
    n_square = c.bit_length() - 1
    nw = nw_ref[...]
    valid = min(c, q_ref.shape[0])
    zeros_c = lambda width: jnp.zeros((c, width), F32)

    def col_of(row):
        return jnp.sum(row * eye_f, axis=1, keepdims=True)

    def conv_silu(x_ref, halo_ref, state_ref, w_ref, ic, lanes, hist_col0):
        lo = ic * c
        if ic == 0:
            cols = slice(hist_col0 + lanes.start, hist_col0 + lanes.stop)
            hist_ref[0:HALO, cols] = jnp.where(pl.program_id(2) == 0, state_ref[0, :, lanes], halo_ref[:, lanes])
            hist_ref[HALO:HALO + valid, cols] = x_ref[0:valid, lanes]
            src, first, src_lanes = hist_ref, HALO - (CONV_W - 1), cols
        else:
            src, first, src_lanes = x_ref, lo - (CONV_W - 1), lanes
        y = sum(src[first + j:first + j + valid, src_lanes] * w_ref[j:j + 1, lanes] for j in range(CONV_W))
        y = y / (1.0 + jnp.exp(-y))
        return y if valid == c else jnp.concatenate([y, jnp.zeros((c - valid, y.shape[1]), F32)], axis=0)

    heads = range(2 * n_qk)
    wq = n_qk * DN_HEAD
    pre = {}
    wave_chunks = max(1, GDN_WAVE_CHAINS // (2 * n_qk))
    for w0 in range(0, n_chunks, wave_chunks):
        ics = range(w0, min(w0 + wave_chunks, n_chunks))
        qkeys = [(ic, j) for ic in ics for j in range(n_qk)]
        keys = [(ic, hv) for ic in ics for hv in heads]
        q, k = {}, {}
        for ic, j in qkeys:
            qk_lanes = slice(j * DN_HEAD, (j + 1) * DN_HEAD)
            q[ic, j] = _l2norm(conv_silu(q_ref, qh_ref, qs_ref, wq_ref, ic, qk_lanes, 0)) * (DN_HEAD ** -0.5)
            k[ic, j] = _l2norm(conv_silu(k_ref, kh_ref, ks_ref, wk_ref, ic, qk_lanes, wq))
        kq, k_t = {}, {}
        for key in qkeys:
            k_t[key] = jnp.concatenate([k[key], zeros_c(DN_HEAD)], axis=0).T
            kq[key] = jnp.dot(jnp.concatenate([k[key], q[key]], axis=0).astype(BF16), k_t[key].astype(BF16),
                              preferred_element_type=F32)
        st, pt = {}, {}
        chunk_row = lambda ic: pl.ds(pl.program_id(2) * n_chunks + ic, 1)
        g_rows = jnp.concatenate([g_ref[0, hv // 2, hv % 2, chunk_row(ic), :] for ic, hv in keys], axis=0)
        gc_rows = g_rows
        for sh in [1 << b for b in range(c.bit_length() - 1)]:
            gc_rows = gc_rows + pltpu.roll(gc_rows, sh, axis=1) * as_f(half_lane[:1] >= sh)
        egc_rows = jnp.exp(gc_rows)
        for ik, (ic, hv) in enumerate(keys):
            j, hh = divmod(hv, 2)
            g_row = g_rows[ik:ik + 1]
            beta_row = beta_ref[0, j, hh, chunk_row(ic), :]
            beta_col = col_of(beta_row)
            gc_col = jnp.sum(g_row * causal_f, axis=1, keepdims=True)
            gc_row = gc_rows[ik:ik + 1]
            egc_row = egc_rows[ik:ik + 1]
            g_last = gc_row[:, c - 1:c]
            decay = jnp.exp(jnp.minimum(gc_col - gc_row, 0.0)) * causal_f
            st[ic, hv] = dict(
                t_scale=hi_f * beta_row, egc_row=egc_row, s_decay=jnp.exp(g_last),
                attn=(kq[ic, j][c:] * decay + eye_hi * egc_row).astype(BF16),
                k_dec_t=(k_t[ic, j] * jnp.exp(g_last - gc_row)).astype(BF16))
            pt[ic, hv] = eye_hi - (beta_col * kq[ic, j][:c] * decay) * strict_f
        for _ in range(n_square):
            for key in keys:
                cur = pt[key]
                pt[key] = jnp.dot(cur.astype(BF16), jnp.concatenate([cur, zeros_c(cw)], axis=0).astype(BF16),
                                  preferred_element_type=F32) + cur * hi_f
        u_base, k_cum = {}, {}
        for ic, hv in keys:
            a = st[ic, hv]
            v = conv_silu(v_ref, vh_ref, vs_ref, wv_ref, ic, slice(hv * DN_HEAD, (hv + 1) * DN_HEAD), 2 * wq)
            t_b = pt[ic, hv] * a["t_scale"]
            below = lambda m: jnp.concatenate([zeros_c(DN_HEAD), m], axis=0).astype(BF16)
            u_base[ic, hv] = jnp.dot(t_b.astype(BF16), below(v), preferred_element_type=F32)
            k_cum[ic, hv] = jnp.dot((t_b * a["egc_row"]).astype(BF16), below(k[ic, hv // 2]),
                                    preferred_element_type=F32)
        for ic, hv in keys:
            a = st[ic, hv]
            pre[ic, hv] = dict(u_base=u_base[ic, hv], attn=a["attn"], k_dec_t=a["k_dec_t"], s_decay=a["s_decay"],
                               kc_q=jnp.concatenate([k_cum[ic, hv], q[ic, hv // 2]], axis=0).astype(BF16))

    s = {hv: s_ref[0, hv] for hv in heads}
    for ic in range(n_chunks):
        ks_qs = {hv: jnp.dot(pre[ic, hv]["kc_q"], s[hv].astype(BF16), preferred_element_type=F32) for hv in heads}
        u = {hv: pre[ic, hv]["u_base"] - ks_qs[hv][:c] for hv in heads}
        o = {hv: jnp.dot(pre[ic, hv]["attn"], jnp.concatenate([u[hv], ks_qs[hv][c:]], axis=0).astype(BF16),
                         preferred_element_type=F32) for hv in heads}
        for hv in heads:
            s[hv] = s[hv] * pre[ic, hv]["s_decay"] + jnp.dot(
                pre[ic, hv]["k_dec_t"], jnp.concatenate([u[hv], zeros_c(DN_HEAD)], axis=0).astype(BF16),
                preferred_element_type=F32)
        for hv in heads:
            lanes = slice(hv * DN_HEAD, (hv + 1) * DN_HEAD)
            on = o[hv] * lax.rsqrt(jnp.mean(o[hv] * o[hv], axis=-1, keepdims=True) + NORM_EPS) * nw
            z = z_ref[ic * c:ic * c + valid, lanes]
            o_ref[ic * c:ic * c + valid, lanes] = (on[:valid] * (z * jax.nn.sigmoid(z))).astype(o_ref.dtype)
    for hv in heads:
        s_ref[0, hv] = s[hv]


def _gated_delta(p_a, row0, bsz, t, conv_state, w_conv, g5, beta5, s0, norm_w, o_init, *, chunk, n_chunks, n_qk):
    rows = min(t, chunk * n_chunks)
    nblk = t // rows
    assert t % rows == 0 and row0 % rows == 0 and (rows == chunk * n_chunks or n_chunks == 1)
    assert rows % HALO == 0
    rb0 = row0 // rows
    wq, wv = DN_HEAD * n_qk, 2 * DN_HEAD * n_qk
    kq0, kv0, kz0 = DN_QK_DIM // wq, 2 * DN_QK_DIM // wv, OFF_Z // wv

    def rows_spec(width, col0):
        return pl.BlockSpec((rows, width), lambda ib, ih, it: (rb0 + ib * nblk + it, col0 + ih))

    def halo_spec(width, col0):
        return pl.BlockSpec((HALO, width), lambda ib, ih, it: (
            jnp.maximum((rb0 + ib * nblk + it) * (rows // HALO) - 1, 0), col0 + ih))

    def state_spec(width, col0):
        return pl.BlockSpec((1, HALO, width), lambda ib, ih, it: (ib, 0, col0 + ih))

    def w_spec(width, col0):
        return pl.BlockSpec((CONV_W, width), lambda ib, ih, it: (0, col0 + ih))

    g_spec = pl.BlockSpec((1, n_qk, 2) + g5.shape[3:], lambda ib, ih, it: (ib, ih, 0, 0, 0))
    s_spec = pl.BlockSpec((1, 2 * n_qk, DN_HEAD, DN_HEAD), lambda ib, ih, it: (ib, ih, 0, 0))
    in_specs = [rows_spec(wq, 0), rows_spec(wq, kq0), rows_spec(wv, kv0), rows_spec(wv, kz0),
                halo_spec(wq, 0), halo_spec(wq, kq0), halo_spec(wv, kv0),
                state_spec(wq, 0), state_spec(wq, kq0), state_spec(wv, kv0),
                w_spec(wq, 0), w_spec(wq, kq0), w_spec(wv, kv0),
                g_spec, g_spec, s_spec, pl.BlockSpec((1, DN_HEAD), lambda ib, ih, it: (0, 0))]
    args = [p_a, p_a, p_a, p_a, p_a, p_a, p_a, conv_state, conv_state, conv_state,
            w_conv, w_conv, w_conv, g5, beta5, s0, norm_w]
    aliases = {}
    if o_init is not None:
        in_specs.append(pl.BlockSpec(memory_space=pl.ANY))
        args.append(o_init)
        aliases = {len(args) - 1: 0}
    vmem = 2 * (2 * rows * wq * 4 + 2 * rows * wv * 4 + rows * wv * 2 + 4 * n_qk * DN_HEAD * DN_HEAD * 4) + (16 << 20)
    return pl.pallas_call(
        functools.partial(_gdn_kernel, chunk=chunk, n_chunks=n_chunks, n_qk=n_qk),
        grid=(bsz, DN_QK_HEADS // n_qk, nblk),
        in_specs=in_specs,
        out_specs=[rows_spec(wv, 0), s_spec],
        out_shape=[jax.ShapeDtypeStruct((p_a.shape[0], DN_V_DIM), BF16),
                   jax.ShapeDtypeStruct((bsz, DN_V_HEADS, DN_HEAD, DN_HEAD), F32)],
        scratch_shapes=[pltpu.VMEM((HALO + min(chunk, rows), 2 * wq + wv), F32)],
        input_output_aliases=aliases,
        compiler_params=_params(3, vmem),
        name="gated_delta_rule",
    )(*args)


def _lru_kernel(x_ref, y_ref, halo_ref, state_ref, wc_ref, bc_ref, wga_ref, bga_ref, wgx_ref, bgx_ref, sp_ref, h0_ref,
                *rest, rows):
    o_ref, hl_ref, a_s, b_s, h_s = rest[-5:]

    @pl.when(pl.program_id(1) == 0)
    def _():
        h_s[...] = h0_ref[0]

    for hd in range(LRU_HEADS):
        lanes = slice(hd * LRU_BW, (hd + 1) * LRU_BW)
        tail = jnp.where(pl.program_id(1) == 0, state_ref[0, :, lanes], halo_ref[:, lanes])
        xh = _causal_conv_rows(tail, x_ref[:, lanes], wc_ref[:, lanes]) + bc_ref[:, lanes]
        xh_bf = xh.astype(BF16)
        r = jax.nn.sigmoid(jnp.dot(xh_bf, wga_ref[hd].astype(BF16), preferred_element_type=F32) + bga_ref[hd])
        i = jax.nn.sigmoid(jnp.dot(xh_bf, wgx_ref[hd].astype(BF16), preferred_element_type=F32) + bgx_ref[hd])
        log_a = (-LRU_C) * r * sp_ref[:, lanes]
        a = jnp.exp(log_a)
        a_s[:, lanes] = a
        b_s[:, lanes] = jnp.sqrt(1.0 - a * a) * (i * xh)

    row = lax.broadcasted_iota(jnp.int32, (SUBLANES, LRU_W), 0)

    def group(ig, h):
        r0 = pl.multiple_of(ig * SUBLANES, SUBLANES)
        a = a_s[pl.ds(r0, SUBLANES), :]
        b = b_s[pl.ds(r0, SUBLANES), :]
        for sh in (1, 2, 4):
            keep = row >= sh
            a_sh = jnp.where(keep, pltpu.roll(a, sh, axis=0), 1.0)
            b_sh = jnp.where(keep, pltpu.roll(b, sh, axis=0), 0.0)
            b = a * b_sh + b
            a = a * a_sh
        hs = a * h + b
        gy = jax.nn.gelu(y_ref[pl.ds(r0, SUBLANES), :])
        o_ref[pl.ds(r0, SUBLANES), :] = (hs * gy).astype(o_ref.dtype)
        return hs[SUBLANES - 1:SUBLANES, :]

    h = lax.fori_loop(0, rows // SUBLANES, group, h_s[...])
    h_s[...] = h
    hl_ref[0] = h


def _rglru(p_r, row0, bsz, t, conv_state, w_conv, b_conv, w_ga, b_ga, w_gx, b_gx, sp, h0, o_init, *, rows):
    w = LRU_W
    nblk = t // rows
    assert t % rows == 0 and rows % SUBLANES == 0 and row0 % rows == 0
    rb0 = row0 // rows
    rows_spec = lambda col: pl.BlockSpec((rows, w), lambda ib, it: (rb0 + ib * nblk + it, col))
    full3 = lambda shape: pl.BlockSpec(shape, lambda ib, it: (0, 0, 0))
    full2 = lambda shape: pl.BlockSpec(shape, lambda ib, it: (0, 0))
    h_spec = pl.BlockSpec((1, 1, w), lambda ib, it: (ib, 0, 0))
    in_specs = [rows_spec(0), rows_spec(1),
                pl.BlockSpec((HALO, w), lambda ib, it: (
                    jnp.maximum((rb0 + ib * nblk + it) * (rows // HALO) - 1, 0), 0)),
                pl.BlockSpec((1, HALO, w), lambda ib, it: (ib, 0, 0)),
                full2((CONV_W, w)), full2((1, w)),
                full3((LRU_HEADS, LRU_BW, LRU_BW)), full3((LRU_HEADS, 1, LRU_BW)),
                full3((LRU_HEADS, LRU_BW, LRU_BW)), full3((LRU_HEADS, 1, LRU_BW)),
                full2((1, w)), h_spec]
    args = [p_r, p_r, p_r, conv_state, w_conv, b_conv, w_ga, b_ga, w_gx, b_gx, sp, h0]
    aliases = {}
    if o_init is not None:
        in_specs.append(pl.BlockSpec(memory_space=pl.ANY))
        args.append(o_init)
        aliases = {len(args) - 1: 0}
    vmem = 2 * (2 * rows * w * 4 + rows * w * 2 + 4 * LRU_HEADS * LRU_BW * LRU_BW * 4) + 2 * rows * w * 4 + (8 << 20)
    return pl.pallas_call(
        functools.partial(_lru_kernel, rows=rows),
        grid=(bsz, nblk),
        in_specs=in_specs,
        out_specs=[rows_spec(0), h_spec],
        out_shape=[jax.ShapeDtypeStruct((p_r.shape[0], w), BF16), jax.ShapeDtypeStruct((bsz, 1, w), F32)],
        scratch_shapes=[pltpu.VMEM((rows, w), F32), pltpu.VMEM((rows, w), F32), pltpu.VMEM((1, w), F32)],
        input_output_aliases=aliases,
        compiler_params=_params(2, vmem),
        name="rg_lru",
    )(*args)


def _merge_kernel(oa_ref, ob_ref, wa_ref, wb_ref, ga_ref, gb_ref, o_ref):
    ya = jnp.dot(oa_ref[...], wa_ref[...].astype(BF16), preferred_element_type=F32)
    yb = jnp.dot(ob_ref[...], wb_ref[...].astype(BF16), preferred_element_type=F32)
    o_ref[...] = (jax.nn.sigmoid(ga_ref[...]) * ya + jax.nn.sigmoid(gb_ref[...]) * yb).astype(o_ref.dtype)


def _merge(oa, ob, wa, wb, gates, ga_col0, gb_col0, *, tm, tn):
    m = oa.shape[0]
    n = wa.shape[1]
    assert m % tm == 0 and n % tn == 0 and ga_col0 % tn == 0 and gb_col0 % tn == 0
    ka, kb = oa.shape[1], ob.shape[1]
    vmem = 2 * (tm * (ka + kb) * 2 + (ka + kb) * tn * 4 + 2 * tm * tn * 4 + tm * tn * 2) + (ka + kb) * tn * 2
    row = lambda kdim: pl.BlockSpec((tm, kdim), lambda i, j: (i, 0))
    col = lambda kdim: pl.BlockSpec((kdim, tn), lambda i, j: (0, j))
    gate = lambda col0: pl.BlockSpec((tm, tn), lambda i, j: (i, col0 // tn + j))
    return pl.pallas_call(
        _merge_kernel,
        grid=(m // tm, n // tn),
        in_specs=[row(ka), row(kb), col(ka), col(kb), gate(ga_col0), gate(gb_col0)],
        out_specs=pl.BlockSpec((tm, tn), lambda i, j: (i, j)),
        out_shape=jax.ShapeDtypeStruct((m, n), BF16),
        compiler_params=_params(2, vmem),
        name="gated_merge",
    )(oa, ob, wa, wb, gates, gates)


def _pack_bf16_pairs(x):
    half = x.shape[1] // 2
    bits = lambda u: pltpu.bitcast(u.astype(BF16).astype(F32), jnp.uint32)
    return (bits(x[:, :half]) >> 16) | (bits(x[:, half:]) & jnp.uint32(0xFFFF0000))


def _unpack_bf16_pairs(words):
    lo = pltpu.bitcast(words << 16, F32)
    hi = pltpu.bitcast(words & jnp.uint32(0xFFFF0000), F32)
    return jnp.concatenate([lo, hi], axis=1).astype(BF16)


def _out_kernel(m_ref, x_ref, w_ref, g_ref, b_ref, wr_ref, br_ref, h_ref, hp_ref, lg_ref):
    y = jnp.dot(m_ref[...], w_ref[...].astype(BF16), preferred_element_type=F32)
    h = _layer_norm(DEEPNORM_ALPHA * x_ref[...] + y, g_ref[...], b_ref[...])
    h_ref[...] = h
    hp_ref[...] = _pack_bf16_pairs(h)
    split = lambda u: (u.astype(BF16), (u - u.astype(BF16).astype(F32)).astype(BF16))
    h_hi, h_lo = split(h)
    w_hi, w_lo = split(wr_ref[...])
    rows = h.shape[0]
    both = jnp.dot(jnp.concatenate([h_hi, h_lo], axis=0), w_hi, preferred_element_type=F32)
    lg_ref[...] = both[:rows] + both[rows:] + jnp.dot(h_hi, w_lo, preferred_element_type=F32) + br_ref[...]


def _out_proj(merged, x, w_out, g, b, w_router, b_router, *, tm):
    m, d = x.shape
    e = w_router.shape[1]
    assert m % tm == 0
    vmem = 2 * (tm * d * 2 + tm * d * 4 + d * d * 4 + tm * d * 6 + d * e * 4 + tm * 128 * 4) + d * d * 2
    rows = lambda width: pl.BlockSpec((tm, width), lambda i: (i, 0))
    full = lambda shape: pl.BlockSpec(shape, lambda i: (0, 0))
    return pl.pallas_call(
        _out_kernel,
        grid=(m // tm,),
        in_specs=[rows(d), rows(d), full((d, d)), full((1, d)), full((1, d)), full((d, e)), full((1, e))],
        out_specs=[rows(d), rows(d // 2), rows(e)],
        out_shape=[jax.ShapeDtypeStruct((m, d), F32), jax.ShapeDtypeStruct((m, d // 2), jnp.uint32),
                   jax.ShapeDtypeStruct((m, e), F32)],
        compiler_params=_params(1, vmem),
        name="out_proj_ln_router",
    )(merged, x, w_out, g, b, w_router, b_router)


def _sub_block_runs(n_valid, lo, hi):
    return (n_valid > lo) & (n_valid <= hi)


def _moe_kernel(ge_ref, gn_ref, gofs_ref, tok_ref, h_hbm, wg_ref, wu_ref, bg_ref, bu_ref, wd_ref, bd_ref, o_ref,
                x_buf, sem):
    ig = pl.program_id(0)
    jf = pl.program_id(1)
    n_valid = gn_ref[ig]
    slot = ig % 2

    n_groups = pl.num_programs(0)
    rows_per_step = MOE_GROUP_ROWS // pl.num_programs(1)

    def start_row(g, buf_slot, r, priority):
        tok = tok_ref[gofs_ref[g] + jnp.minimum(r, jnp.maximum(gn_ref[g] - 1, 0))]
        pltpu.make_async_copy(h_hbm.at[pl.ds(tok, 1), :], x_buf.at[buf_slot, pl.ds(r, 1), :],
                              sem.at[buf_slot]).start(priority=priority)

    def wait_rows(buf_slot):
        pltpu.make_async_copy(h_hbm.at[pl.ds(0, MOE_GROUP_ROWS), :], x_buf.at[buf_slot], sem.at[buf_slot]).wait()

    @pl.when(jf == 0)
    def _():
        @pl.when(ig == 0)
        def _():
            def body(ib, carry):
                for u in range(DMA_ISSUE_UNROLL):
                    start_row(0, 0, ib * DMA_ISSUE_UNROLL + u, u % 2)
                return carry
            lax.fori_loop(0, MOE_GROUP_ROWS // DMA_ISSUE_UNROLL, body, 0)

        @pl.when((ig == 0) | (gn_ref[jnp.maximum(ig - 1, 0)] > 0))
        def _():
            wait_rows(slot)

    @pl.when((ig == n_groups - 1) & (jf == pl.num_programs(1) - 1) & (n_valid > 0))
    def _():
        wait_rows(1 - slot)

    @pl.when((n_valid == 0) & (jf == 0))
    def _():
        o_ref[...] = jnp.zeros(o_ref.shape, o_ref.dtype)

    @pl.when(n_valid > 0)
    def _():
        @pl.when(jf == 0)
        def _():
            o_ref[...] = jnp.broadcast_to(bd_ref[0], o_ref.shape)

        for lo, hi, row0, n_rows in MOE_SUB_BLOCKS:
            @pl.when(_sub_block_runs(n_valid, lo, hi))
            def _():
                assert row0 == 0
                wg = wg_ref[0].astype(BF16)
                wu = wu_ref[0].astype(BF16)
                wd = wd_ref[0].astype(BF16)
                chains = [slice(r, min(r + MOE_CHAIN_ROWS, n_rows)) for r in range(0, n_rows, MOE_CHAIN_ROWS)]
                xb = [_unpack_bf16_pairs(x_buf[slot, rows, :]) for rows in chains]
                hg = [jnp.dot(x, wg, preferred_element_type=F32) + bg_ref[0] for x in xb]
                hu = [jnp.dot(x, wu, preferred_element_type=F32) + bu_ref[0] for x in xb]
                act = []
                for g, u in zip(hg, hu):
                    g = jnp.minimum(g, SWIGLU_LIMIT)
                    u = jnp.clip(u, -SWIGLU_LIMIT, SWIGLU_LIMIT)
                    act.append(((u + 1.0) * g * jax.nn.sigmoid(SWIGLU_ALPHA * g)).astype(BF16))
                g_next = jnp.minimum(ig + 1, n_groups - 1)
                share = MOE_GROUP_ROWS // (D_FF // MOE_FF_TILE) // len(chains)
                for ic, (rows, a) in enumerate(zip(chains, act)):
                    o_ref[rows, :] += jnp.dot(a, wd, preferred_element_type=F32)
                    if row0 == 0:
                        for u in range(ic * share, (ic + 1) * share):
                            start_row(g_next, 1 - slot, jf * rows_per_step + u, u % 2)


def _moe_experts(h_packed, group_expert, group_valid, group_offset, sorted_tok, w_gu, b_gu, w_dn, b_dn):
    n_groups = group_expert.shape[0]
    d = 2 * h_packed.shape[1]
    nf = D_FF // MOE_FF_TILE
    gm, tf = MOE_GROUP_ROWS, MOE_FF_TILE

    def f_eff(ig, jf, gn):
        return jnp.where(gn[ig] > 0, jf, nf - 1)

    grid_spec = pltpu.PrefetchScalarGridSpec(
        num_scalar_prefetch=4,
        grid=(n_groups, nf),
        in_specs=[
            pl.BlockSpec(memory_space=pl.ANY),
            pl.BlockSpec((1, d, tf), lambda ig, jf, ge, gn, go, tk: (ge[ig], 0, f_eff(ig, jf, gn))),
            pl.BlockSpec((1, d, tf), lambda ig, jf, ge, gn, go, tk: (ge[ig], 0, nf + f_eff(ig, jf, gn))),
            pl.BlockSpec((1, 1, tf), lambda ig, jf, ge, gn, go, tk: (ge[ig], 0, f_eff(ig, jf, gn))),
            pl.BlockSpec((1, 1, tf), lambda ig, jf, ge, gn, go, tk: (ge[ig], 0, nf + f_eff(ig, jf, gn))),
            pl.BlockSpec((1, tf, d), lambda ig, jf, ge, gn, go, tk: (ge[ig], f_eff(ig, jf, gn), 0)),
            pl.BlockSpec((1, 1, d), lambda ig, jf, ge, gn, go, tk: (ge[ig], 0, 0)),
        ],
        out_specs=pl.BlockSpec((gm, d), lambda ig, jf, ge, gn, go, tk: (ig, 0)),
        scratch_shapes=[pltpu.VMEM((2, gm, d // 2), jnp.uint32), pltpu.SemaphoreType.DMA((2,))],
    )
    vmem = 2 * gm * (d // 2) * 4 + 2 * (3 * d * tf * 4 + gm * d * 4) + 3 * d * tf * 2 + (6 << 20)
    return pl.pallas_call(
        _moe_kernel,
        grid_spec=grid_spec,
        out_shape=jax.ShapeDtypeStruct((n_groups * gm, d), F32),
        compiler_params=_params(2, vmem),
        name="moe_experts",
    )(group_expert, group_valid, group_offset, sorted_tok, h_packed, w_gu, w_gu, b_gu, b_gu, w_dn, b_dn)


def _combine_kernel(dest_ref, rows_hbm, gate_ref, h_ref, g_ref, b_ref, oa_ref, ob_ref, buf, sem, *, tm, steps_a):
    i = pl.program_id(0)
    n_steps = pl.num_programs(0)

    tokens_per_batch = DMA_ISSUE_UNROLL // TOP_K

    def start_rows(step, buf_slot):
        def body(ib, carry):
            for u in range(DMA_ISSUE_UNROLL):
                t = ib * tokens_per_batch + u // TOP_K
                kk = u % TOP_K
                src = dest_ref[(step * tm + t) * TOP_K + kk]
                pltpu.make_async_copy(rows_hbm.at[pl.ds(src, 1), :], buf.at[buf_slot, pl.ds(kk * tm + t, 1), :],
                                      sem.at[buf_slot]).start(priority=u % 2)
            return carry
        lax.fori_loop(0, tm // tokens_per_batch, body, 0)

    @pl.when(i == 0)
    def _():
        start_rows(0, 0)

    @pl.when(i + 1 < n_steps)
    def _():
        start_rows(i + 1, (i + 1) % 2)

    slot = i % 2
    pltpu.make_async_copy(rows_hbm.at[pl.ds(0, TOP_K * tm), :], buf.at[slot], sem.at[slot]).wait()
    y = sum(gate_ref[:, kk:kk + 1] * buf[slot, kk * tm:(kk + 1) * tm, :] for kk in range(TOP_K))
    out = _layer_norm(DEEPNORM_ALPHA * h_ref[...] + y, g_ref[...], b_ref[...])

    @pl.when(i < steps_a)
    def _():
        oa_ref[...] = out

    @pl.when(i >= steps_a)
    def _():
        ob_ref[...] = out


def _combine(out_rows, dest, gate, h, g, b, *, tm, n_first):
    n, d = h.shape
    assert n % tm == 0 and n_first % tm == 0 and 0 < n_first < n
    steps_a = n_first // tm
    grid_spec = pltpu.PrefetchScalarGridSpec(
        num_scalar_prefetch=1,
        grid=(n // tm,),
        in_specs=[pl.BlockSpec(memory_space=pl.ANY),
                  pl.BlockSpec((tm, TOP_K), lambda i, dest: (i, 0)),
                  pl.BlockSpec((tm, d), lambda i, dest: (i, 0)),
                  pl.BlockSpec((1, d), lambda i, dest: (0, 0)),
                  pl.BlockSpec((1, d), lambda i, dest: (0, 0))],
        out_specs=[pl.BlockSpec((tm, d), lambda i, dest: (jnp.minimum(i, steps_a - 1), 0)),
                   pl.BlockSpec((tm, d), lambda i, dest: (jnp.maximum(i - steps_a, 0), 0))],
        scratch_shapes=[pltpu.VMEM((2, TOP_K * tm, d), F32), pltpu.SemaphoreType.DMA((2,))],
    )
    vmem = 2 * TOP_K * tm * d * 4 + 2 * (3 * tm * d * 4 + tm * 128 * 4) + (4 << 20)
    return pl.pallas_call(
        functools.partial(_combine_kernel, tm=tm, steps_a=steps_a),
        grid_spec=grid_spec,
        out_shape=[jax.ShapeDtypeStruct((n_first, d), F32), jax.ShapeDtypeStruct((n - n_first, d), F32)],
        compiler_params=_params(1, vmem),
        name="moe_combine_ln",
    )(dest, out_rows, gate, h, g, b)


def _moe_ffn_ln(h, h_packed, logits, w_gu, b_gu, w_dn, b_dn, ln_g, ln_b, n_first):
    n, d = h.shape
    nk = n * TOP_K
    gm = MOE_GROUP_ROWS
    n_groups = N_EXPERTS + nk // gm
    top_logit, top_e = lax.top_k(logits, TOP_K)
    gate = jax.nn.softmax(top_logit, axis=-1)
    flat_e = top_e.reshape(nk)
    onehot = (flat_e[:, None] == jnp.arange(N_EXPERTS, dtype=flat_e.dtype)[None, :]).astype(jnp.int32)
    rank = jnp.take_along_axis(jnp.cumsum(onehot, axis=0), flat_e[:, None], axis=1)[:, 0] - 1
    counts = jnp.sum(onehot, axis=0)
    start_e = jnp.cumsum(counts) - counts
    groups_e = (counts + gm - 1) // gm
    gend_e = jnp.cumsum(groups_e)
    gstart_e = gend_e - groups_e
    dest = (gstart_e[flat_e] * gm + rank).astype(jnp.int32)
    sorted_tok = (jnp.argsort(flat_e, stable=True) // TOP_K).astype(jnp.int32)
    n_used = gend_e[-1]
    gidx = jnp.arange(n_groups, dtype=jnp.int32)
    gclamp = jnp.minimum(gidx, n_used - 1)
    group_expert = jnp.minimum(jnp.searchsorted(gend_e, gclamp, side="right"), N_EXPERTS - 1).astype(jnp.int32)
    within = gclamp - gstart_e[group_expert]
    used = gidx < n_used
    group_valid = jnp.where(used, jnp.clip(counts[group_expert] - within * gm, 0, gm), 0).astype(jnp.int32)
    group_offset = jnp.where(used, start_e[group_expert] + within * gm, 0).astype(jnp.int32)
    out_rows = _moe_experts(h_packed, group_expert, group_valid, group_offset, sorted_tok,
                            w_gu, b_gu.reshape(N_EXPERTS, 1, -1), w_dn, b_dn.reshape(N_EXPERTS, 1, -1))
    return _combine(out_rows, dest, gate, h, ln_g, ln_b, tm=COMBINE_ROWS, n_first=n_first)


def _pad_state(state):
    return jnp.pad(state, ((0, 0), (HALO - (CONV_W - 1), 0), (0, 0)))


def _last_rows(p, width, row0, bsz, t):
    if bsz == 1:
        return p[row0 + t - (CONV_W - 1):row0 + t, :width][None]
    return p[row0:row0 + bsz * t, :width].reshape(bsz, t, width)[:, t - (CONV_W - 1):]


def _dn_gates(p_ba, row0, bsz, t, chunk, a_log, dt_bias):
    ba = p_ba[row0:row0 + bsz * t, :2 * DN_V_HEADS].reshape(bsz, t, 2 * DN_V_HEADS)
    beta = jax.nn.sigmoid(ba[..., :DN_V_HEADS])
    g = -jnp.exp(a_log) * jax.nn.softplus(ba[..., DN_V_HEADS:] + dt_bias)
    n_ch = -(-t // chunk)

    def to5(u):
        u = jnp.pad(u, ((0, 0), (0, n_ch * chunk - t), (0, 0)))
        u = jnp.transpose(u, (0, 2, 1)).reshape(bsz, DN_QK_HEADS, 2, n_ch, chunk)
        return jnp.concatenate([u, u], axis=-1)

    return to5(g), to5(beta)


def kernel(x_prompt, x_sample, state_dn_conv, state_dn_ssm, state_lru_conv, state_lru_h, w_in, w_dn_conv, dn_a_log, dn_dt_bias, dn_norm_w, w_lru_conv, b_lru_conv, w_lru_ga, b_lru_ga, w_lru_gx, b_lru_gx, lru_lambda, w_proj_a, w_proj_b, w_out, ln1_g, ln1_b, w_router, b_router, w_gate_up, b_gate_up, w_down, b_down, ln2_g, ln2_b):
    assert w_in.shape[0] == 1, "single layer"
    bp, tp, d = x_prompt.shape
    bs, ts, _ = x_sample.shape
    np_, ns = bp * tp, bs * ts
    n = np_ + ns

    x_all = jnp.concatenate([x_prompt.reshape(np_, d), x_sample.reshape(ns, d)], axis=0)
    x_bf = x_all.astype(BF16)
    w_in_t = jnp.swapaxes(w_in[0], 0, 1)
    tm = n // 5
    p_a = _matmul_nt(x_bf, w_in_t, tm=tm, tn=1024, n_cols=OFF_B)
    p_ba = _matmul_nt(x_bf, w_in_t, tm=tm, tn=LANES, col0=OFF_B, n_cols=LANES)
    p_r = _matmul_nt(x_bf, w_in_t, tm=tm, tn=1024, col0=OFF_LX)

    zeros = lambda *shape: jnp.zeros(shape, F32)
    norm_w = dn_norm_w[0].reshape(1, DN_HEAD)
    g5, beta5 = _dn_gates(p_ba, 0, bp, tp, GDN_CHUNK, dn_a_log[0], dn_dt_bias[0])
    o_a, dn_s_p = _gated_delta(
        p_a, 0, bp, tp, zeros(bp, HALO, DN_QKV_DIM), w_dn_conv[0], g5, beta5,
        zeros(bp, DN_V_HEADS, DN_HEAD, DN_HEAD), norm_w, jnp.zeros((n, DN_V_DIM), BF16),
        chunk=GDN_CHUNK, n_chunks=GDN_CHUNKS_PER_STEP, n_qk=GDN_QK_HEADS_PER_STEP)
    g5, beta5 = _dn_gates(p_ba, np_, bs, ts, GDN_CHUNK, dn_a_log[0], dn_dt_bias[0])
    o_a, dn_s_s = _gated_delta(
        p_a, np_, bs, ts, _pad_state(state_dn_conv[0]), w_dn_conv[0], g5, beta5, state_dn_ssm[0], norm_w, o_a,
        chunk=GDN_CHUNK, n_chunks=1, n_qk=4)

    lru_args = (w_lru_conv[0], b_lru_conv, w_lru_ga[0], b_lru_ga[0].reshape(LRU_HEADS, 1, LRU_BW),
                w_lru_gx[0], b_lru_gx[0].reshape(LRU_HEADS, 1, LRU_BW),
                jax.nn.softplus(-lru_lambda[0]).reshape(1, LRU_W))
    o_b, h_p = _rglru(p_r, 0, bp, tp, zeros(bp, HALO, LRU_W), *lru_args, zeros(bp, 1, LRU_W),
                      jnp.zeros((n, LRU_W), BF16), rows=256)
    o_b, h_s = _rglru(p_r, np_, bs, ts, _pad_state(state_lru_conv[0]), *lru_args,
                      state_lru_h[0].reshape(bs, 1, LRU_W), o_b, rows=ts)

    merged = _merge(o_a, o_b, w_proj_a[0], w_proj_b[0], p_r, 2 * LRU_W, 3 * LRU_W, tm=n // 10, tn=256)
    h, h_packed, logits = _out_proj(merged, x_all, w_out[0], ln1_g, ln1_b, w_router[0], b_router, tm=n // 26)
    y_p, y_s = _moe_ffn_ln(h, h_packed, logits, w_gate_up[0], b_gate_up[0], w_down[0], b_down[0], ln2_g, ln2_b, np_)

    return (y_p.reshape(bp, tp, d), y_s.reshape(bs, ts, d),
            _last_rows(p_a, DN_QKV_DIM, 0, bp, tp)[None], dn_s_p[None],
            _last_rows(p_r, LRU_W, 0, bp, tp)[None], h_p.reshape(bp, LRU_W)[None],
            _last_rows(p_a, DN_QKV_DIM, np_, bs, ts)[None], dn_s_s[None],
            _last_rows(p_r, LRU_W, np_, bs, ts)[None], h_s.reshape(bs, LRU_W)[None])
```
